```python
import math
import jax
import jax.numpy as jnp
from jax import lax
import numpy as np

D_MODEL = 1024
BATCH = 4
SEQ = 4096
DEPTH = 2
DEC_BATCH = 16
DEC_SEQ = 32
PAST_LEN = 2048

CHUNK = 64
N_MIXERS = 2
N_DIFF_LAYERS = (DEPTH + 1) // 2
N_RET_LAYERS = DEPTH // 2
DA_HEADS = 4
DA_HEAD_DIM = D_MODEL // (2 * DA_HEADS)
DA_QBLOCK = 128
DA_SUBLN_EPS = 1e-5
RET_HEADS = 4
RET_DK = D_MODEL // RET_HEADS
RET_DV = 2 * RET_DK
RET_GN_EPS = 1e-6
ROPE_BASE = 10000.0
D_FF = 2816
NORM_EPS = 1e-6
NEG_INF = -1e30

kernel_name = "chunk_streaming_diffattn_retention_macaron"


def rmsnorm(x, g, eps=NORM_EPS):
    xf = x.astype(jnp.float32)
    y = xf * lax.rsqrt(jnp.mean(xf * xf, axis=-1, keepdims=True) + eps)
    return (y * g.astype(jnp.float32)).astype(x.dtype)


def swiglu_half(x, g, w_gate, w_up, w_down):
    h = rmsnorm(x, g)
    u = jax.nn.silu(h @ w_gate) * (h @ w_up)
    return x + 0.5 * (u @ w_down)


def lambda_init(layer_idx):
    return 0.8 - 0.6 * math.exp(-0.3 * layer_idx)


def diff_attn_project(h, w_qkv):
    B, T, _ = h.shape
    q, k, v = jnp.split(h @ w_qkv, 3, axis=-1)
    q = q.reshape(B, T, DA_HEADS, 2, DA_HEAD_DIM)
    k = k.reshape(B, T, DA_HEADS, 2, DA_HEAD_DIM)
    v = v.reshape(B, T, DA_HEADS, 2 * DA_HEAD_DIM)
    return q, k, v


def diff_attn_block(q, k, v, q_pos, k_pos, lam):
    s = jnp.einsum('bqhcd,bkhcd->bhcqk', q.astype(jnp.float32), k.astype(jnp.float32)) * (DA_HEAD_DIM ** -0.5)
    mask = (k_pos[None, :] // CHUNK) <= (q_pos[:, None] // CHUNK)
    s = jnp.where(mask[None, None, None], s, NEG_INF)
    p = jax.nn.softmax(s, axis=-1)
    a = p[:, :, 0] - lam * p[:, :, 1]
    return jnp.einsum('bhqk,bkhe->bqhe', a, v.astype(jnp.float32))


def diff_attn_prompt(q, k, v, lam):
    B, T = q.shape[0], q.shape[1]
    nb = T // DA_QBLOCK
    qb = jnp.moveaxis(q.reshape(B, nb, DA_QBLOCK, DA_HEADS, 2, DA_HEAD_DIM), 1, 0)
    posb = jnp.arange(T, dtype=jnp.int32).reshape(nb, DA_QBLOCK)
    k_pos = jnp.arange(T, dtype=jnp.int32)
    o = lax.map(lambda args: diff_attn_block(args[0], k, v, args[1], k_pos, lam), (qb, posb))
    return jnp.moveaxis(o, 0, 1).reshape(B, T, DA_HEADS, 2 * DA_HEAD_DIM)


def diff_attn_out(o, lam_init, subln, w_o):
    B, T = o.shape[0], o.shape[1]
    o = rmsnorm(o, subln, DA_SUBLN_EPS) * (1.0 - lam_init)
    return o.reshape(B, T, DA_HEADS * 2 * DA_HEAD_DIM).astype(w_o.dtype) @ w_o


def rotary(x, pos):
    dk = x.shape[-1]
    angle = 1.0 / (ROPE_BASE ** jnp.linspace(0.0, 1.0, dk // 2, dtype=jnp.float32))
    angle = jnp.repeat(angle, 2)
    theta = pos[:, None] * angle[None, :]
    sin = jnp.sin(theta)[None, :, None, :]
    cos = jnp.cos(theta)[None, :, None, :]
    x1 = x[..., 0::2]
    x2 = x[..., 1::2]
    rot = jnp.stack((-x2, x1), axis=-1).reshape(x.shape)
    return x * cos + rot * sin


def ret_log_decay():
    return jnp.log1p(-jnp.exp2(-5.0 - jnp.arange(RET_HEADS, dtype=jnp.float32)))


def ret_project(h, w_in, pos):
    B, T, _ = h.shape
    nq = RET_HEADS * RET_DK
    nv = RET_HEADS * RET_DV
    proj = h @ w_in
    q, k, v, g = jnp.split(proj, [nq, 2 * nq, 2 * nq + nv], axis=-1)
    q = rotary(q.reshape(B, T, RET_HEADS, RET_DK).astype(jnp.float32), pos)
    k = rotary(k.reshape(B, T, RET_HEADS, RET_DK).astype(jnp.float32), pos) * (RET_DK ** -0.5)
    v = v.reshape(B, T, RET_HEADS, RET_DV).astype(jnp.float32)
    return q, k, v, g


def retention_chunk(S, q, k, v):
    L = q.shape[1]
    lg = ret_log_decay()
    idx = jnp.arange(L, dtype=jnp.float32)
    rel = idx[:, None] - idx[None, :]
    dmask = jnp.where(rel[None] >= 0, jnp.exp(jnp.maximum(rel, 0.0)[None] * lg[:, None, None]), 0.0)
    qk = jnp.einsum('bihd,bjhd->bhij', q, k) * dmask[None]
    inner = jnp.einsum('bhij,bjhe->bihe', qk, v)
    cross = jnp.einsum('bihd,bhde->bihe', q, S) * jnp.exp((idx + 1.0)[:, None] * lg[None, :])[None, :, :, None]
    kd = k * jnp.exp((L - 1.0 - idx)[:, None] * lg[None, :])[None, :, :, None]
    S_new = S * jnp.exp(L * lg)[None, :, None, None] + jnp.einsum('bjhd,bjhe->bhde', kd, v)
    return S_new, inner + cross


def retention_prompt(q, k, v):
    B, T = q.shape[0], q.shape[1]
    nc = T // CHUNK

    def to_chunks(a):
        return jnp.moveaxis(a.reshape(B, nc, CHUNK, *a.shape[2:]), 1, 0)

    S0 = jnp.zeros((B, RET_HEADS, RET_DK, RET_DV), jnp.float32)
    S_fin, o = lax.scan(lambda S, c: retention_chunk(S, c[0], c[1], c[2]), S0,
                        (to_chunks(q), to_chunks(k), to_chunks(v)))
    return S_fin, jnp.moveaxis(o, 0, 1).reshape(B, T, RET_HEADS, RET_DV)


def ret_out(o, g, w_o):
    B, T = o.shape[0], o.shape[1]
    on = o * lax.rsqrt(jnp.mean(o * o, axis=-1, keepdims=True) + RET_GN_EPS)
    y = jax.nn.silu(g.astype(jnp.float32)) * on.reshape(B, T, RET_HEADS * RET_DV)
    return y.astype(w_o.dtype) @ w_o


def setup_inputs(seed: int = 0) -> dict:
    key = jax.random.key(seed)
    ks = jax.random.split(key, 32)
    f32 = jnp.float32

    def nrm(k, shape, scale):
        return jax.random.normal(k, shape, f32) * scale

    def gain(k, shape):
        return 1.0 + 0.01 * jax.random.normal(k, shape, f32)

    D = D_MODEL
    return {
        "x_prompt": nrm(ks[0], (BATCH, SEQ, D), 1.0),
        "x_sample": nrm(ks[1], (DEC_BATCH, DEC_SEQ, D), 1.0),
        "cache_diff_k": nrm(ks[2], (N_DIFF_LAYERS, DEC_BATCH, PAST_LEN, DA_HEADS, 2, DA_HEAD_DIM), 1.0),
        "cache_diff_v": nrm(ks[3], (N_DIFF_LAYERS, DEC_BATCH, PAST_LEN, DA_HEADS, 2 * DA_HEAD_DIM), 1.0),
        "state_ret": nrm(ks[4], (N_RET_LAYERS, DEC_BATCH, RET_HEADS, RET_DK, RET_DV), 0.3),
        "ffn1_norm": gain(ks[5], (DEPTH, D)),
        "ffn1_w_gate": nrm(ks[6], (DEPTH, D, D_FF), D ** -0.5),
        "ffn1_w_up": nrm(ks[7], (DEPTH, D, D_FF), D ** -0.5),
        "ffn1_w_down": nrm(ks[8], (DEPTH, D_FF, D), D_FF ** -0.5),
        "mix_norm": gain(ks[9], (DEPTH, D)),
        "da_w_qkv": nrm(ks[10], (N_DIFF_LAYERS, D, 3 * D), D ** -0.5),
        "da_lambda_q1": nrm(ks[11], (N_DIFF_LAYERS, DA_HEAD_DIM), 0.1),
        "da_lambda_k1": nrm(ks[12], (N_DIFF_LAYERS, DA_HEAD_DIM), 0.1),
        "da_lambda_q2": nrm(ks[13], (N_DIFF_LAYERS, DA_HEAD_DIM), 0.1),
        "da_lambda_k2": nrm(ks[14], (N_DIFF_LAYERS, DA_HEAD_DIM), 0.1),
        "da_subln": gain(ks[15], (N_DIFF_LAYERS, 2 * DA_HEAD_DIM)),
        "da_w_o": nrm(ks[16], (N_DIFF_LAYERS, D, D), D ** -0.5),
        "ret_w_in": nrm(ks[17], (N_RET_LAYERS, D, 2 * RET_HEADS * RET_DK + 2 * RET_HEADS * RET_DV), D ** -0.5),
        "ret_w_o": nrm(ks[18], (N_RET_LAYERS, RET_HEADS * RET_DV, D), (RET_HEADS * RET_DV) ** -0.5),
        "ffn2_norm": gain(ks[19], (DEPTH, D)),
        "ffn2_w_gate": nrm(ks[20], (DEPTH, D, D_FF), D ** -0.5),
        "ffn2_w_up": nrm(ks[21], (DEPTH, D, D_FF), D ** -0.5),
        "ffn2_w_down": nrm(ks[22], (DEPTH, D_FF, D), D_FF ** -0.5),
        "final_norm": gain(ks[23], (D,)),
    }


def reference(x_prompt, x_sample, cache_diff_k, cache_diff_v, state_ret,
              ffn1_norm, ffn1_w_gate, ffn1_w_up, ffn1_w_down, mix_norm,
              da_w_qkv, da_lambda_q1, da_lambda_k1, da_lambda_q2, da_lambda_k2, da_subln, da_w_o,
              ret_w_in, ret_w_o,
              ffn2_norm, ffn2_w_gate, ffn2_w_up, ffn2_w_down, final_norm):
    f32 = jnp.float32
    t_p = x_prompt.shape[1]
    t_s = x_sample.shape[1]
    past = cache_diff_k.shape[2]
    pos_p = jnp.arange(t_p, dtype=f32)
    pos_s = past + jnp.arange(t_s, dtype=f32)

    xp, xs = x_prompt, x_sample
    kp_list, vp_list, sp_list = [], [], []
    ks_list, vs_list, ss_list = [], [], []
    for i in range(DEPTH):
        xp = swiglu_half(xp, ffn1_norm[i], ffn1_w_gate[i], ffn1_w_up[i], ffn1_w_down[i])
        xs = swiglu_half(xs, ffn1_norm[i], ffn1_w_gate[i], ffn1_w_up[i], ffn1_w_down[i])
        hp = rmsnorm(xp, mix_norm[i])
        hs = rmsnorm(xs, mix_norm[i])
        if i % N_MIXERS == 0:
            a = i // N_MIXERS
            lam_i = lambda_init(i)
            lam = (jnp.exp(jnp.sum(da_lambda_q1[a].astype(f32) * da_lambda_k1[a].astype(f32)))
                   - jnp.exp(jnp.sum(da_lambda_q2[a].astype(f32) * da_lambda_k2[a].astype(f32))) + lam_i)
            qp, kp, vp = diff_attn_project(hp, da_w_qkv[a])
            op = diff_attn_prompt(qp, kp, vp, lam)
            qs, ks, vs = diff_attn_project(hs, da_w_qkv[a])
            k_all = jnp.concatenate([cache_diff_k[a], ks.astype(cache_diff_k.dtype)], axis=1)
            v_all = jnp.concatenate([cache_diff_v[a], vs.astype(cache_diff_v.dtype)], axis=1)
            k_pos = jnp.arange(past + t_s, dtype=jnp.int32)
            q_pos = past + jnp.arange(t_s, dtype=jnp.int32)
            os_ = diff_attn_block(qs, k_all, v_all, q_pos, k_pos, lam)
            xp = xp + diff_attn_out(op, lam_i, da_subln[a], da_w_o[a]).astype(xp.dtype)
            xs = xs + diff_attn_out(os_, lam_i, da_subln[a], da_w_o[a]).astype(xs.dtype)
            kp_list.append(kp)
            vp_list.append(vp)
            ks_list.append(ks)
            vs_list.append(vs)
        else:
            r = i // N_MIXERS
            qp, kp, vp, gp = ret_project(hp, ret_w_in[r], pos_p)
            sp, op = retention_prompt(qp, kp, vp)
            qs, ks, vs, gs = ret_project(hs, ret_w_in[r], pos_s)
            ss, os_ = retention_chunk(state_ret[r].astype(f32), qs, ks, vs)
            xp = xp + ret_out(op, gp, ret_w_o[r]).astype(xp.dtype)
            xs = xs + ret_out(os_, gs, ret_w_o[r]).astype(xs.dtype)
            sp_list.append(sp)
            ss_list.append(ss)
        xp = swiglu_half(xp, ffn2_norm[i], ffn2_w_gate[i], ffn2_w_up[i], ffn2_w_down[i])
        xs = swiglu_half(xs, ffn2_norm[i], ffn2_w_gate[i], ffn2_w_up[i], ffn2_w_down[i])

    y_prompt = rmsnorm(xp, final_norm)
    y_sample = rmsnorm(xs, final_norm)
    return (y_prompt, y_sample,
            jnp.stack(kp_list), jnp.stack(vp_list), jnp.stack(sp_list),
            jnp.stack(ks_list), jnp.stack(vs_list), jnp.stack(ss_list))
```

```python
import functools
import math

import jax
import jax.numpy as jnp
from jax import lax
from jax.experimental import pallas as pl
from jax.experimental.pallas import tpu as pltpu

F32 = jnp.float32
BF16 = jnp.bfloat16

NORM_EPS = 1e-6
DA_SUBLN_EPS = 1e-5
RET_GN_EPS = 1e-6
ROPE_BASE = 10000.0
CHUNK = 64
NEG_INF = -1e30

V7X_VMEM_BYTES = 64 * 1024 * 1024
VMEM_LIMIT_BYTES = (V7X_VMEM_BYTES * 3) // 4

ROW_TILE = 512
ATTN_TILE = 256
RET_TILE = 256


def _params(*semantics):
    return pltpu.CompilerParams(dimension_semantics=semantics,
                                vmem_limit_bytes=VMEM_LIMIT_BYTES)


def _resident(shape):
    return pl.BlockSpec(shape, lambda *_: (0,) * len(shape),
                        pipeline_mode=pl.Buffered(1))


def _rmsnorm(x, g, eps):
    return x * lax.rsqrt(jnp.mean(x * x, axis=-1, keepdims=True) + eps) * g


def _dot(a, b):
    return jnp.dot(a, b, preferred_element_type=F32)


def _dot_nt(a, b):
    return lax.dot_general(a, b, (((1,), (1,)), ((), ())), preferred_element_type=F32)


def _dot_tn(a, b):
    return lax.dot_general(a, b, (((0,), (0,)), ((), ())), preferred_element_type=F32)


def _row_tile(n):
    return ROW_TILE if n % ROW_TILE == 0 else n


def _ffn_kernel(x_ref, g_ref, wg_ref, wu_ref, wd_ref, *rest, f_tile, final):
    if final:
        fg_ref, o_ref = rest
    else:
        (o_ref,) = rest
    x = x_ref[...]
    h = _rmsnorm(x, g_ref[...], NORM_EPS).astype(BF16)
    acc = None
    for c in range(wg_ref.shape[1] // f_tile):
        sl = slice(c * f_tile, (c + 1) * f_tile)
        gate = _dot(h, wg_ref[:, sl])
        up = _dot(h, wu_ref[:, sl])
        a = (gate * jax.nn.sigmoid(gate) * up).astype(BF16)
        d = _dot(a, wd_ref[sl, :])
        acc = d if acc is None else acc + d
    y = x + 0.5 * acc
    if final:
        y = _rmsnorm(y, fg_ref[...], NORM_EPS)
    o_ref[...] = y


def _ffn(x, g, wg, wu, wd, final_g=None):
    n, d = x.shape
    f = wg.shape[1]
    tm = _row_tile(n)
    f_tile = 256 if f % 256 == 0 else f
    final = final_g is not None
    row = pl.BlockSpec((tm, d), lambda i: (i, 0))
    in_specs = [row, _resident((1, d)), _resident((d, f)), _resident((d, f)), _resident((f, d))]
    args = [x, g.reshape(1, d), wg, wu, wd]
    if final:
        in_specs.append(_resident((1, d)))
        args.append(final_g.reshape(1, d))
    return pl.pallas_call(
        functools.partial(_ffn_kernel, f_tile=f_tile, final=final),
        out_shape=jax.ShapeDtypeStruct((n, d), F32),
        grid=(n // tm,),
        in_specs=in_specs,
        out_specs=row,
        compiler_params=_params("parallel"),
        name="ffn_final" if final else "ffn",
    )(*args)


def _da_proj_kernel(x_ref, g_ref, w_ref, q_ref, k_ref, v_ref, kb_ref, vb_ref, *, q_scale):
    d = x_ref.shape[1]
    h = _rmsnorm(x_ref[...], g_ref[...], NORM_EPS).astype(BF16)
    q_ref[...] = (_dot(h, w_ref[:, 0:d]) * q_scale).astype(BF16)
    k = _dot(h, w_ref[:, d:2 * d])
    k_ref[...] = k
    kb_ref[...] = k.astype(BF16)
    v = _dot(h, w_ref[:, 2 * d:3 * d])
    v_ref[...] = v
    vb_ref[...] = v.astype(BF16)


def _da_proj(x, g, w, q_scale):
    n, d = x.shape
    tm = _row_tile(n)
    row = pl.BlockSpec((tm, d), lambda i: (i, 0))
    return pl.pallas_call(
        functools.partial(_da_proj_kernel, q_scale=q_scale),
        out_shape=(jax.ShapeDtypeStruct((n, d), BF16), jax.ShapeDtypeStruct((n, d), F32),
                   jax.ShapeDtypeStruct((n, d), F32), jax.ShapeDtypeStruct((n, d), BF16),
                   jax.ShapeDtypeStruct((n, d), BF16)),
        grid=(n // tm,),
        in_specs=[row, _resident((1, d)), _resident((d, 3 * d))],
        out_specs=(row, row, row, row, row),
        compiler_params=_params("parallel"),
        name="da_proj",
    )(x, g.reshape(1, d), w)


def _da_lambda(lq1_ref, lk1_ref, lq2_ref, lk2_ref, lam_init):
    s1 = jnp.sum(lq1_ref[...] * lk1_ref[...], axis=1, keepdims=True)
    s2 = jnp.sum(lq2_ref[...] * lk2_ref[...], axis=1, keepdims=True)
    return jnp.exp(s1) - jnp.exp(s2) + lam_init


def _da_head_out(o, sub_ref, lam_init):
    return (_rmsnorm(o, sub_ref[...], DA_SUBLN_EPS) * (1.0 - lam_init)).astype(BF16)


def _softmax_step(s, vv, m, l, acc):
    m_new = jnp.maximum(m, jnp.max(s, axis=1, keepdims=True))
    alpha = jnp.exp(m - m_new)
    p = jnp.exp(s - m_new)
    l_new = alpha * l + jnp.sum(p, axis=1, keepdims=True)
    acc_new = alpha * acc + _dot(p.astype(BF16), vv)
    return m_new, l_new, acc_new


def _da_attn_kernel(lq1_ref, lk1_ref, lq2_ref, lk2_ref, q_ref, k_ref, v_ref, x_ref, wo_ref,
                    sub_ref, o_ref, cat_ref, *, n_heads, dh, tq, lam_init):
    i = pl.program_id(1)
    lam = _da_lambda(lq1_ref, lk1_ref, lq2_ref, lk2_ref, lam_init)
    row_chunk = lax.broadcasted_iota(jnp.int32, (tq, tq), 0) // CHUNK
    col_chunk = lax.broadcasted_iota(jnp.int32, (tq, tq), 1) // CHUNK
    diag_mask = col_chunk <= row_chunk

    for h in range(n_heads):
        c0 = h * 2 * dh
        q1 = q_ref[0, :, c0:c0 + dh]
        q2 = q_ref[0, :, c0 + dh:c0 + 2 * dh]

        def step(j, carry, masked, c0=c0, q1=q1, q2=q2):
            ks = pl.multiple_of(j * tq, tq)
            k1 = k_ref[0, pl.ds(ks, tq), c0:c0 + dh]
            k2 = k_ref[0, pl.ds(ks, tq), c0 + dh:c0 + 2 * dh]
            vv = v_ref[0, pl.ds(ks, tq), c0:c0 + 2 * dh]
            s1 = _dot_nt(q1, k1)
            s2 = _dot_nt(q2, k2)
            if masked:
                s1 = jnp.where(diag_mask, s1, NEG_INF)
                s2 = jnp.where(diag_mask, s2, NEG_INF)
            m1, l1, a1, m2, l2, a2 = carry
            return _softmax_step(s1, vv, m1, l1, a1) + _softmax_step(s2, vv, m2, l2, a2)

        m0 = jnp.full((tq, 1), NEG_INF, F32)
        l0 = jnp.zeros((tq, 1), F32)
        a0 = jnp.zeros((tq, 2 * dh), F32)
        carry = lax.fori_loop(0, i, functools.partial(step, masked=False),
                              (m0, l0, a0, m0, l0, a0))
        _, l1, a1, _, l2, a2 = step(i, carry, masked=True)
        o = a1 / l1 - lam * (a2 / l2)
        cat_ref[:, c0:c0 + 2 * dh] = _da_head_out(o, sub_ref, lam_init)

    o_ref[0] = x_ref[0] + _dot(cat_ref[...], wo_ref[...])


def _da_attn_prompt(lams, q, kb, vb, x, wo, subln, n_heads, lam_init):
    b, t, d = x.shape
    dh = d // (2 * n_heads)
    tq = ATTN_TILE if t % ATTN_TILE == 0 else t
    assert tq % CHUNK == 0
    blk = pl.BlockSpec((1, tq, d), lambda bi, i: (bi, i, 0))
    seq = pl.BlockSpec((1, t, d), lambda bi, i: (bi, 0, 0), pipeline_mode=pl.Buffered(1))
    lam_spec = _resident((1, dh))
    return pl.pallas_call(
        functools.partial(_da_attn_kernel, n_heads=n_heads, dh=dh, tq=tq, lam_init=lam_init),
        out_shape=jax.ShapeDtypeStruct((b, t, d), F32),
        grid=(b, t // tq),
        in_specs=[lam_spec] * 4 + [blk, seq, seq, blk, _resident((d, d)), _resident((1, 2 * dh))],
        out_specs=blk,
        scratch_shapes=[pltpu.VMEM((tq, d), BF16)],
        compiler_params=_params("parallel", "arbitrary"),
        name="da_attn_prompt",
    )(*lams, q, kb, vb, x, wo, subln.reshape(1, 2 * dh))


def _da_attn_sample_kernel(lq1_ref, lk1_ref, lq2_ref, lk2_ref, q_ref, kc_ref, vc_ref, kn_ref,
                           vn_ref, x_ref, wo_ref, sub_ref, o_ref, cat_ref, *, n_heads, dh,
                           lam_init):
    lam = _da_lambda(lq1_ref, lk1_ref, lq2_ref, lk2_ref, lam_init)
    for h in range(n_heads):
        c0 = h * 2 * dh
        probs = []
        for c in (c0, c0 + dh):
            qh = q_ref[0, :, c:c + dh]
            sc = _dot_nt(qh, kc_ref[0, :, c:c + dh].astype(BF16))
            sn = _dot_nt(qh, kn_ref[0, :, c:c + dh])
            m = jnp.maximum(jnp.max(sc, axis=1, keepdims=True), jnp.max(sn, axis=1, keepdims=True))
            pc = jnp.exp(sc - m)
            pn = jnp.exp(sn - m)
            l = jnp.sum(pc, axis=1, keepdims=True) + jnp.sum(pn, axis=1, keepdims=True)
            probs.append((pc / l, pn / l))
        ac = (probs[0][0] - lam * probs[1][0]).astype(BF16)
        an = (probs[0][1] - lam * probs[1][1]).astype(BF16)
        o = (_dot(ac, vc_ref[0, :, c0:c0 + 2 * dh].astype(BF16))
             + _dot(an, vn_ref[0, :, c0:c0 + 2 * dh]))
        cat_ref[:, c0:c0 + 2 * dh] = _da_head_out(o, sub_ref, lam_init)
    o_ref[0] = x_ref[0] + _dot(cat_ref[...], wo_ref[...])


def _da_attn_sample(lams, q, k_cache, v_cache, kb, vb, x, wo, subln, n_heads, lam_init):
    b, ts, d = x.shape
    past = k_cache.shape[1]
    assert past % CHUNK == 0 and ts <= CHUNK
    dh = d // (2 * n_heads)
    new = pl.BlockSpec((1, ts, d), lambda bi: (bi, 0, 0))
    cache = pl.BlockSpec((1, past, d), lambda bi: (bi, 0, 0))
    lam_spec = _resident((1, dh))
    return pl.pallas_call(
        functools.partial(_da_attn_sample_kernel, n_heads=n_heads, dh=dh, lam_init=lam_init),
        out_shape=jax.ShapeDtypeStruct((b, ts, d), F32),
        grid=(b,),
        in_specs=[lam_spec] * 4 + [new, cache, cache, new, new, new, _resident((d, d)),
                                   _resident((1, 2 * dh))],
        out_specs=new,
        scratch_shapes=[pltpu.VMEM((ts, d), BF16)],
        compiler_params=_params("parallel"),
        name="da_attn_sample",
    )(*lams, q, k_cache, v_cache, kb, vb, x, wo, subln.reshape(1, 2 * dh))


def _rope_tables(pos, dk):
    angle = 1.0 / (ROPE_BASE ** jnp.linspace(0.0, 1.0, dk // 2, dtype=F32))
    angle = jnp.repeat(angle, 2)
    theta = pos[:, None] * angle[None, :]
    sign = jnp.where(jnp.arange(dk) % 2 == 0, -1.0, 1.0).astype(F32)
    return jnp.cos(theta), jnp.sin(theta) * sign[None, :]


def _ret_proj_kernel(x_ref, g_ref, w_ref, cos_ref, sin_ref, q_ref, k_ref, v_ref, gate_ref, *,
                     n_heads, dk, k_scale):
    nq = n_heads * dk
    nv = v_ref.shape[1]
    h = _rmsnorm(x_ref[...], g_ref[...], NORM_EPS).astype(BF16)
    cos = cos_ref[...]
    sin = sin_ref[...]
    even = lax.broadcasted_iota(jnp.int32, cos.shape, 1) % 2 == 0

    def rotary(a):
        rot = jnp.where(even, pltpu.roll(a, dk - 1, 1), pltpu.roll(a, 1, 1))
        return a * cos + rot * sin

    q = _dot(h, w_ref[:, 0:nq])
    k = _dot(h, w_ref[:, nq:2 * nq])
    for hd in range(n_heads):
        sl = slice(hd * dk, (hd + 1) * dk)
        q_ref[:, sl] = rotary(q[:, sl]).astype(BF16)
        k_ref[:, sl] = (rotary(k[:, sl]) * k_scale).astype(BF16)
    v_ref[...] = _dot(h, w_ref[:, 2 * nq:2 * nq + nv]).astype(BF16)
    gate_ref[...] = _dot(h, w_ref[:, 2 * nq + nv:2 * nq + 2 * nv])


def _ret_proj(x, g, w, cos, sin, n_heads, dk, dv):
    n, d = x.shape
    nq, nv = n_heads * dk, n_heads * dv
    tm = _row_tile(cos.shape[0])
    pos_blocks = cos.shape[0] // tm
    row = lambda width: pl.BlockSpec((tm, width), lambda i: (i, 0))
    tab = pl.BlockSpec((tm, dk), lambda i: (i % pos_blocks, 0))
    return pl.pallas_call(
        functools.partial(_ret_proj_kernel, n_heads=n_heads, dk=dk, k_scale=dk ** -0.5),
        out_shape=(jax.ShapeDtypeStruct((n, nq), BF16), jax.ShapeDtypeStruct((n, nq), BF16),
                   jax.ShapeDtypeStruct((n, nv), BF16), jax.ShapeDtypeStruct((n, nv), F32)),
        grid=(n // tm,),
        in_specs=[row(d), _resident((1, d)), _resident((d, 2 * nq + 2 * nv)), tab, tab],
        out_specs=(row(nq), row(nq), row(nv), row(nv)),
        compiler_params=_params("parallel"),
        name="ret_proj",
    )(x, g.reshape(1, d), w, cos, sin)


def _ret_core_kernel(lg_ref, q_ref, k_ref, v_ref, gate_ref, x_ref, s0_ref, wo_ref, o_ref,
                     sfin_ref, s_scr, y_scr, *, n_heads, dk, dv, blk):
    c = pl.program_id(1)

    @pl.when(c == 0)
    def _():
        s_scr[...] = s0_ref[0]

    row = lax.broadcasted_iota(jnp.int32, (blk, blk), 0)
    col = lax.broadcasted_iota(jnp.int32, (blk, blk), 1)
    rel = (row - col).astype(F32)
    idx = lax.broadcasted_iota(jnp.int32, (blk, 1), 0).astype(F32)

    for h in range(n_heads):
        lg = lg_ref[h]
        dmask = jnp.where(rel >= 0, jnp.exp(jnp.maximum(rel, 0.0) * lg), 0.0)
        qh = q_ref[0, :, h * dk:(h + 1) * dk]
        kh = k_ref[0, :, h * dk:(h + 1) * dk]
        vh = v_ref[0, :, h * dv:(h + 1) * dv]
        s = s_scr[h]
        qk = _dot_nt(qh, kh) * dmask
        inner = _dot(qk.astype(BF16), vh)
        cross = _dot(qh, s.astype(BF16)) * jnp.exp((idx + 1.0) * lg)
        o = inner + cross
        kd = (kh.astype(F32) * jnp.exp((blk - 1.0 - idx) * lg)).astype(BF16)
        s_decay = jnp.exp(jnp.full((1, dv), float(blk), F32) * lg)
        s_scr[h] = s * s_decay + _dot_tn(kd, vh)
        on = o * lax.rsqrt(jnp.mean(o * o, axis=-1, keepdims=True) + RET_GN_EPS)
        gate = gate_ref[0, :, h * dv:(h + 1) * dv]
        y_scr[:, h * dv:(h + 1) * dv] = (gate * jax.nn.sigmoid(gate) * on).astype(BF16)

    o_ref[0] = x_ref[0] + _dot(y_scr[...], wo_ref[...])

    @pl.when(c == pl.num_programs(1) - 1)
    def _():
        sfin_ref[0] = s_scr[...]


def _ret_core(lg, q, k, v, gate, x, s0, wo, n_heads, dk, dv):
    b, t, d = x.shape
    blk = RET_TILE if t % RET_TILE == 0 else t
    nq, nv = n_heads * dk, n_heads * dv
    tok = lambda width: pl.BlockSpec((1, blk, width), lambda bi, c: (bi, c, 0))
    state = pl.BlockSpec((1, n_heads, dk, dv), lambda bi, c: (bi, 0, 0, 0))
    return pl.pallas_call(
        functools.partial(_ret_core_kernel, n_heads=n_heads, dk=dk, dv=dv, blk=blk),
        out_shape=(jax.ShapeDtypeStruct((b, t, d), F32),
                   jax.ShapeDtypeStruct((b, n_heads, dk, dv), F32)),
        grid=(b, t // blk),
        in_specs=[pl.BlockSpec(memory_space=pltpu.SMEM), tok(nq), tok(nq), tok(nv), tok(nv),
                  tok(d), state, _resident((nv, d))],
        out_specs=(tok(d), state),
        scratch_shapes=[pltpu.VMEM((n_heads, dk, dv), F32), pltpu.VMEM((blk, nv), BF16)],
        compiler_params=_params("parallel", "arbitrary"),
        name="ret_core",
    )(lg, q, k, v, gate, x, s0, wo)


def _lambda_init(layer_idx):
    return 0.8 - 0.6 * math.exp(-0.3 * layer_idx)


def kernel(x_prompt, x_sample, cache_diff_k, cache_diff_v, state_ret, ffn1_norm, ffn1_w_gate, ffn1_w_up, ffn1_w_down, mix_norm, da_w_qkv, da_lambda_q1, da_lambda_k1, da_lambda_q2, da_lambda_k2, da_subln, da_w_o, ret_w_in, ret_w_o, ffn2_norm, ffn2_w_gate, ffn2_w_up, ffn2_w_down, final_norm):
    bp, tp, d = x_prompt.shape
    bs, ts, _ = x_sample.shape
    past = cache_diff_k.shape[2]
    da_heads = cache_diff_k.shape[3]
    dh = cache_diff_k.shape[5]
    ret_heads, dk, dv = state_ret.shape[2:]
    depth = ffn1_norm.shape[0]
    bf = lambda w: w.astype(BF16)

    xp = x_prompt.reshape(bp * tp, d)
    xs = x_sample.reshape(bs * ts, d)
    kp_list, vp_list, sp_list = [], [], []
    ks_list, vs_list, ss_list = [], [], []
    for i in range(depth):
        w1 = (ffn1_norm[i], bf(ffn1_w_gate[i]), bf(ffn1_w_up[i]), bf(ffn1_w_down[i]))
        xp = _ffn(xp, *w1)
        xs = _ffn(xs, *w1)
        if i % 2 == 0:
            a = i // 2
            lam_init = _lambda_init(i)
            w_qkv, w_o = bf(da_w_qkv[a]), bf(da_w_o[a])
            lams = [v[a].reshape(1, dh) for v in (da_lambda_q1, da_lambda_k1, da_lambda_q2, da_lambda_k2)]
            qp, kp, vp, kpb, vpb = _da_proj(xp, mix_norm[i], w_qkv, dh ** -0.5)
            qs, ks, vs, ksb, vsb = _da_proj(xs, mix_norm[i], w_qkv, dh ** -0.5)
            seq = lambda z, b, t: z.reshape(b, t, d)
            xp = _da_attn_prompt(lams, seq(qp, bp, tp), seq(kpb, bp, tp), seq(vpb, bp, tp),
                                 seq(xp, bp, tp), w_o, da_subln[a], da_heads,
                                 lam_init).reshape(bp * tp, d)
            xs = _da_attn_sample(lams, seq(qs, bs, ts), cache_diff_k[a].reshape(bs, past, d),
                                 cache_diff_v[a].reshape(bs, past, d), seq(ksb, bs, ts),
                                 seq(vsb, bs, ts), seq(xs, bs, ts), w_o, da_subln[a], da_heads,
                                 lam_init).reshape(bs * ts, d)
            kp_list.append(kp.reshape(bp, tp, da_heads, 2, dh))
            vp_list.append(vp.reshape(bp, tp, da_heads, 2 * dh))
            ks_list.append(ks.reshape(bs, ts, da_heads, 2, dh))
            vs_list.append(vs.reshape(bs, ts, da_heads, 2 * dh))
        else:
            r = i // 2
            w_in, w_o = bf(ret_w_in[r]), bf(ret_w_o[r])
            lg = jnp.log1p(-jnp.exp2(-5.0 - jnp.arange(ret_heads, dtype=F32)))
            nq, nv = ret_heads * dk, ret_heads * dv
            groups = ((xp, bp, tp, jnp.arange(tp, dtype=F32),
                       jnp.zeros((bp, ret_heads, dk, dv), F32), sp_list),
                      (xs, bs, ts, past + jnp.arange(ts, dtype=F32),
                       state_ret[r].astype(F32), ss_list))
            outs = []
            for x, b, t, pos, s0, s_list in groups:
                cos, sin = _rope_tables(pos, dk)
                if t % ROW_TILE != 0:
                    cos, sin = jnp.tile(cos, (b, 1)), jnp.tile(sin, (b, 1))
                q, k, v, gate = _ret_proj(x, mix_norm[i], w_in, cos, sin, ret_heads, dk, dv)
                y, s_fin = _ret_core(lg, q.reshape(b, t, nq), k.reshape(b, t, nq),
                                     v.reshape(b, t, nv), gate.reshape(b, t, nv),
                                     x.reshape(b, t, d), s0, w_o, ret_heads, dk, dv)
                outs.append(y.reshape(b * t, d))
                s_list.append(s_fin)
            xp, xs = outs
        fin = final_norm if i == depth - 1 else None
        w2 = (ffn2_norm[i], bf(ffn2_w_gate[i]), bf(ffn2_w_up[i]), bf(ffn2_w_down[i]))
        xp = _ffn(xp, *w2, final_g=fin)
        xs = _ffn(xs, *w2, final_g=fin)

    return (xp.reshape(bp, tp, d), xs.reshape(bs, ts, d),
            jnp.stack(kp_list), jnp.stack(vp_list), jnp.stack(sp_list),
            jnp.stack(ks_list), jnp.stack(vs_list), jnp.stack(ss_list))
```

```python
import functools
import math

import jax
import jax.numpy as jnp
from jax import lax
from jax.experimental import pallas as pl
from jax.experimental.pallas import tpu as pltpu

F32 = jnp.float32
BF16 = jnp.bfloat16

NORM_EPS = 1e-6
DA_SUBLN_EPS = 1e-5
RET_GN_EPS = 1e-6
ROPE_BASE = 10000.0
CHUNK = 64
NEG_INF = -1e30

V7X_VMEM_BYTES = 64 * 1024 * 1024
VMEM_LIMIT_BYTES = (V7X_VMEM_BYTES * 3) // 4

ROW_TILE = 512
ATTN_TILE = 512
RET_TILE = 256


def _params(*semantics):
    return pltpu.CompilerParams(dimension_semantics=semantics,
                                vmem_limit_bytes=VMEM_LIMIT_BYTES)


def _resident(shape):
    return pl.BlockSpec(shape, lambda *_: (0,) * len(shape),
                        pipeline_mode=pl.Buffered(1))


def _rmsnorm(x, g, eps):
    return x * lax.rsqrt(jnp.mean(x * x, axis=-1, keepdims=True) + eps) * g


def _dot(a, b):
    return jnp.dot(a, b, preferred_element_type=F32)


def _dot_nt(a, b):
    return lax.dot_general(a, b, (((1,), (1,)), ((), ())), preferred_element_type=F32)


def _dot_tn(a, b):
    return lax.dot_general(a, b, (((0,), (0,)), ((), ())), preferred_element_type=F32)


def _row_tile(n):
    return ROW_TILE if n % ROW_TILE == 0 else n


def _ffn_kernel(x_ref, g_ref, wg_ref, wu_ref, wd_ref, *rest, f_tile, final):
    if final:
        fg_ref, o_ref = rest
    else:
        (o_ref,) = rest
    x = x_ref[...]
    h = _rmsnorm(x, g_ref[...], NORM_EPS).astype(BF16)
    acc = None
    for c in range(wg_ref.shape[1] // f_tile):
        sl = slice(c * f_tile, (c + 1) * f_tile)
        gate = _dot(h, wg_ref[:, sl])
        up = _dot(h, wu_ref[:, sl])
        a = (gate * jax.nn.sigmoid(gate) * up).astype(BF16)
        d = _dot(a, wd_ref[sl, :])
        acc = d if acc is None else acc + d
    y = x + 0.5 * acc
    if final:
        y = _rmsnorm(y, fg_ref[...], NORM_EPS)
    o_ref[...] = y


def _ffn(x, g, wg, wu, wd, final_g=None):
    n, d = x.shape
    f = wg.shape[1]
    tm = _row_tile(n)
    f_tile = 256 if f % 256 == 0 else f
    final = final_g is not None
    row = pl.BlockSpec((tm, d), lambda i: (i, 0))
    in_specs = [row, _resident((1, d)), _resident((d, f)), _resident((d, f)), _resident((f, d))]
    args = [x, g.reshape(1, d), wg, wu, wd]
    if final:
        in_specs.append(_resident((1, d)))
        args.append(final_g.reshape(1, d))
    return pl.pallas_call(
        functools.partial(_ffn_kernel, f_tile=f_tile, final=final),
        out_shape=jax.ShapeDtypeStruct((n, d), F32),
        grid=(n // tm,),
        in_specs=in_specs,
        out_specs=row,
        compiler_params=_params("parallel"),
        name="ffn_final" if final else "ffn",
    )(*args)


def _da_proj_kernel(x_ref, g_ref, w_ref, wvt_ref, q_ref, k_ref, v_ref, kb_ref, vb_ref, *,
                    q_scale, transposed_v):
    d = x_ref.shape[1]
    h = _rmsnorm(x_ref[...], g_ref[...], NORM_EPS).astype(BF16)
    q_ref[...] = (_dot(h, w_ref[:, 0:d]) * q_scale).astype(BF16)
    k = _dot(h, w_ref[:, d:2 * d])
    k_ref[...] = k
    kb_ref[...] = k.astype(BF16)
    v = _dot(h, w_ref[:, 2 * d:3 * d])
    v_ref[...] = v
    if transposed_v:
        vb_ref[...] = _dot_nt(wvt_ref[...], h).astype(BF16)
    else:
        vb_ref[...] = v.astype(BF16)


def _da_proj(x, g, w, wvt, q_scale, transposed_v):
    n, d = x.shape
    tm = _row_tile(n)
    row = pl.BlockSpec((tm, d), lambda i: (i, 0))
    if transposed_v:
        vb_shape, vb_spec = (d, n), pl.BlockSpec((d, tm), lambda i: (0, i))
    else:
        vb_shape, vb_spec = (n, d), row
    return pl.pallas_call(
        functools.partial(_da_proj_kernel, q_scale=q_scale, transposed_v=transposed_v),
        out_shape=(jax.ShapeDtypeStruct((n, d), BF16), jax.ShapeDtypeStruct((n, d), F32),
                   jax.ShapeDtypeStruct((n, d), F32), jax.ShapeDtypeStruct((n, d), BF16),
                   jax.ShapeDtypeStruct(vb_shape, BF16)),
        grid=(n // tm,),
        in_specs=[row, _resident((1, d)), _resident((d, 3 * d)), _resident((d, d))],
        out_specs=(row, row, row, row, vb_spec),
        compiler_params=_params("parallel"),
        name="da_proj_t" if transposed_v else "da_proj",
    )(x, g.reshape(1, d), w, wvt)


def _da_lambda(lq1_ref, lk1_ref, lq2_ref, lk2_ref, lam_init):
    s1 = jnp.sum(lq1_ref[...] * lk1_ref[...], axis=1, keepdims=True)
    s2 = jnp.sum(lq2_ref[...] * lk2_ref[...], axis=1, keepdims=True)
    return jnp.exp(s1) - jnp.exp(s2) + lam_init


def _da_head_out(o, sub_ref, lam_init):
    return (_rmsnorm(o, sub_ref[...], DA_SUBLN_EPS) * (1.0 - lam_init)).astype(BF16)


def _da_attn_kernel(lq1_ref, lk1_ref, lq2_ref, lk2_ref, q_ref, k_ref, vt_ref, x_ref, wo_ref,
                    sub_ref, o_ref, acc_ref, cat_ref, *, n_heads, dh, tq, lam_init):
    i = pl.program_id(1)
    lam = _da_lambda(lq1_ref, lk1_ref, lq2_ref, lk2_ref, lam_init)
    key_chunk = lax.broadcasted_iota(jnp.int32, (tq, tq), 0) // CHUNK
    qry_chunk = lax.broadcasted_iota(jnp.int32, (tq, tq), 1) // CHUNK
    diag_mask = key_chunk <= qry_chunk

    for h in range(n_heads):
        c0 = h * 2 * dh
        qs = (q_ref[0, :, c0:c0 + dh], q_ref[0, :, c0 + dh:c0 + 2 * dh])
        acc_ref[...] = jnp.zeros_like(acc_ref)

        def scores(j, c0=c0, qs=qs):
            ks = pl.multiple_of(j * tq, tq)
            return tuple(_dot_nt(k_ref[0, pl.ds(ks, tq), c0 + c * dh:c0 + (c + 1) * dh], qs[c])
                         for c in range(2))

        def absorb(j, stats, s_pair, masked, c0=c0):
            ks = pl.multiple_of(j * tq, tq)
            vt = vt_ref[c0:c0 + 2 * dh, pl.ds(ks, tq)]
            new = []
            for c in range(2):
                m, l = stats[2 * c], stats[2 * c + 1]
                s = jnp.where(diag_mask, s_pair[c], NEG_INF) if masked else s_pair[c]
                m_new = jnp.maximum(m, jnp.max(s, axis=0, keepdims=True))
                alpha = jnp.exp(m - m_new)
                p = jnp.exp(s - m_new)
                acc_ref[c] = alpha * acc_ref[c] + _dot(vt, p.astype(BF16))
                new += [m_new, alpha * l + jnp.sum(p, axis=0, keepdims=True)]
            return tuple(new)

        def step(j, carry):
            stats, s_pair = carry
            s_next = scores(j + 1)
            return absorb(j, stats, s_pair, masked=False), s_next

        m0 = jnp.full((1, tq), NEG_INF, F32)
        l0 = jnp.zeros((1, tq), F32)
        stats, s_diag = lax.fori_loop(0, i, step, ((m0, l0, m0, l0), scores(0)))
        _, l1, _, l2 = absorb(i, stats, s_diag, masked=True)
        o = acc_ref[0] / l1 - lam * (acc_ref[1] / l2)
        on = o * lax.rsqrt(jnp.mean(o * o, axis=0, keepdims=True) + DA_SUBLN_EPS) * sub_ref[...]
        cat_ref[c0:c0 + 2 * dh, :] = (on * (1.0 - lam_init)).astype(BF16)

    o_ref[0] = x_ref[0] + _dot_tn(cat_ref[...], wo_ref[...])


def _da_attn_prompt(lams, q, kb, vt, x, wo, subln, n_heads, lam_init):
    b, t, d = x.shape
    dh = d // (2 * n_heads)
    tq = ATTN_TILE if t % ATTN_TILE == 0 else t
    assert tq % CHUNK == 0
    blk = pl.BlockSpec((1, tq, d), lambda bi, i: (bi, i, 0))
    seq = pl.BlockSpec((1, t, d), lambda bi, i: (bi, 0, 0), pipeline_mode=pl.Buffered(1))
    seq_t = pl.BlockSpec((d, t), lambda bi, i: (0, bi), pipeline_mode=pl.Buffered(1))
    lam_spec = _resident((1, dh))
    return pl.pallas_call(
        functools.partial(_da_attn_kernel, n_heads=n_heads, dh=dh, tq=tq, lam_init=lam_init),
        out_shape=jax.ShapeDtypeStruct((b, t, d), F32),
        grid=(b, t // tq),
        in_specs=[lam_spec] * 4 + [blk, seq, seq_t, blk, _resident((d, d)),
                                   _resident((2 * dh, 1))],
        out_specs=blk,
        scratch_shapes=[pltpu.VMEM((2, 2 * dh, tq), F32), pltpu.VMEM((d, tq), BF16)],
        compiler_params=_params("parallel", "arbitrary"),
        name="da_attn_prompt",
    )(*lams, q, kb, vt, x, wo, subln.reshape(2 * dh, 1))


def _da_attn_sample_kernel(lq1_ref, lk1_ref, lq2_ref, lk2_ref, q_ref, kc_ref, vc_ref, kn_ref,
                           vn_ref, x_ref, wo_ref, sub_ref, o_ref, cat_ref, *, n_heads, dh,
                           lam_init):
    lam = _da_lambda(lq1_ref, lk1_ref, lq2_ref, lk2_ref, lam_init)
    for h in range(n_heads):
        c0 = h * 2 * dh
        probs = []
        for c in (c0, c0 + dh):
            qh = q_ref[0, :, c:c + dh]
            sc = _dot_nt(qh, kc_ref[0, :, c:c + dh].astype(BF16))
            sn = _dot_nt(qh, kn_ref[0, :, c:c + dh])
            m = jnp.maximum(jnp.max(sc, axis=1, keepdims=True), jnp.max(sn, axis=1, keepdims=True))
            pc = jnp.exp(sc - m)
            pn = jnp.exp(sn - m)
            l = jnp.sum(pc, axis=1, keepdims=True) + jnp.sum(pn, axis=1, keepdims=True)
            probs.append((pc / l, pn / l))
        ac = (probs[0][0] - lam * probs[1][0]).astype(BF16)
        an = (probs[0][1] - lam * probs[1][1]).astype(BF16)
        o = (_dot(ac, vc_ref[0, :, c0:c0 + 2 * dh].astype(BF16))
             + _dot(an, vn_ref[0, :, c0:c0 + 2 * dh]))
        cat_ref[:, c0:c0 + 2 * dh] = _da_head_out(o, sub_ref, lam_init)
    o_ref[0] = x_ref[0] + _dot(cat_ref[...], wo_ref[...])


def _da_attn_sample(lams, q, k_cache, v_cache, kb, vb, x, wo, subln, n_heads, lam_init):
    b, ts, d = x.shape
    past = k_cache.shape[1]
    assert past % CHUNK == 0 and ts <= CHUNK
    dh = d // (2 * n_heads)
    new = pl.BlockSpec((1, ts, d), lambda bi: (bi, 0, 0))
    cache = pl.BlockSpec((1, past, d), lambda bi: (bi, 0, 0))
    lam_spec = _resident((1, dh))
    return pl.pallas_call(
        functools.partial(_da_attn_sample_kernel, n_heads=n_heads, dh=dh, lam_init=lam_init),
        out_shape=jax.ShapeDtypeStruct((b, ts, d), F32),
        grid=(b,),
        in_specs=[lam_spec] * 4 + [new, cache, cache, new, new, new, _resident((d, d)),
                                   _resident((1, 2 * dh))],
        out_specs=new,
        scratch_shapes=[pltpu.VMEM((ts, d), BF16)],
        compiler_params=_params("parallel"),
        name="da_attn_sample",
    )(*lams, q, k_cache, v_cache, kb, vb, x, wo, subln.reshape(1, 2 * dh))


def _rope_tables(pos, dk):
    angle = 1.0 / (ROPE_BASE ** jnp.linspace(0.0, 1.0, dk // 2, dtype=F32))
    angle = jnp.repeat(angle, 2)
    theta = pos[:, None] * angle[None, :]
    sign = jnp.where(jnp.arange(dk) % 2 == 0, -1.0, 1.0).astype(F32)
    return jnp.cos(theta), jnp.sin(theta) * sign[None, :]


def _ret_proj_kernel(x_ref, g_ref, w_ref, cos_ref, sin_ref, q_ref, k_ref, v_ref, gate_ref, *,
                     n_heads, dk, k_scale):
    nq = n_heads * dk
    nv = v_ref.shape[1]
    h = _rmsnorm(x_ref[...], g_ref[...], NORM_EPS).astype(BF16)
    cos = cos_ref[...]
    sin = sin_ref[...]
    even = lax.broadcasted_iota(jnp.int32, cos.shape, 1) % 2 == 0

    def rotary(a):
        rot = jnp.where(even, pltpu.roll(a, dk - 1, 1), pltpu.roll(a, 1, 1))
        return a * cos + rot * sin

    q = _dot(h, w_ref[:, 0:nq])
    k = _dot(h, w_ref[:, nq:2 * nq])
    for hd in range(n_heads):
        sl = slice(hd * dk, (hd + 1) * dk)
        q_ref[:, sl] = rotary(q[:, sl]).astype(BF16)
        k_ref[:, sl] = (rotary(k[:, sl]) * k_scale).astype(BF16)
    v_ref[...] = _dot(h, w_ref[:, 2 * nq:2 * nq + nv]).astype(BF16)
    gate_ref[...] = _dot(h, w_ref[:, 2 * nq + nv:2 * nq + 2 * nv])


def _ret_proj(x, g, w, cos, sin, n_heads, dk, dv):
    n, d = x.shape
    nq, nv = n_heads * dk, n_heads * dv
    tm = _row_tile(cos.shape[0])
    pos_blocks = cos.shape[0] // tm
    row = lambda width: pl.BlockSpec((tm, width), lambda i: (i, 0))
    tab = pl.BlockSpec((tm, dk), lambda i: (i % pos_blocks, 0))
    return pl.pallas_call(
        functools.partial(_ret_proj_kernel, n_heads=n_heads, dk=dk, k_scale=dk ** -0.5),
        out_shape=(jax.ShapeDtypeStruct((n, nq), BF16), jax.ShapeDtypeStruct((n, nq), BF16),
                   jax.ShapeDtypeStruct((n, nv), BF16), jax.ShapeDtypeStruct((n, nv), F32)),
        grid=(n // tm,),
        in_specs=[row(d), _resident((1, d)), _resident((d, 2 * nq + 2 * nv)), tab, tab],
        out_specs=(row(nq), row(nq), row(nv), row(nv)),
        compiler_params=_params("parallel"),
        name="ret_proj",
    )(x, g.reshape(1, d), w, cos, sin)


def _ret_core_kernel(lg_ref, q_ref, k_ref, v_ref, gate_ref, x_ref, s0_ref, wo_ref, o_ref,
                     sfin_ref, s_scr, y_scr, *, n_heads, dk, dv, blk):
    c = pl.program_id(1)

    @pl.when(c == 0)
    def _():
        s_scr[...] = s0_ref[0]

    row = lax.broadcasted_iota(jnp.int32, (blk, blk), 0)
    col = lax.broadcasted_iota(jnp.int32, (blk, blk), 1)
    rel = (row - col).astype(F32)
    idx = lax.broadcasted_iota(jnp.int32, (blk, 1), 0).astype(F32)

    for h in range(n_heads):
        lg = lg_ref[h]
        dmask = jnp.where(rel >= 0, jnp.exp(jnp.maximum(rel, 0.0) * lg), 0.0)
        qh = q_ref[0, :, h * dk:(h + 1) * dk]
        kh = k_ref[0, :, h * dk:(h + 1) * dk]
        vh = v_ref[0, :, h * dv:(h + 1) * dv]
        s = s_scr[h]
        qk = _dot_nt(qh, kh) * dmask
        inner = _dot(qk.astype(BF16), vh)
        cross = _dot(qh, s.astype(BF16)) * jnp.exp((idx + 1.0) * lg)
        o = inner + cross
        kd = (kh.astype(F32) * jnp.exp((blk - 1.0 - idx) * lg)).astype(BF16)
        s_decay = jnp.exp(jnp.full((1, dv), float(blk), F32) * lg)
        s_scr[h] = s * s_decay + _dot_tn(kd, vh)
        on = o * lax.rsqrt(jnp.mean(o * o, axis=-1, keepdims=True) + RET_GN_EPS)
        gate = gate_ref[0, :, h * dv:(h + 1) * dv]
        y_scr[:, h * dv:(h + 1) * dv] = (gate * jax.nn.sigmoid(gate) * on).astype(BF16)

    o_ref[0] = x_ref[0] + _dot(y_scr[...], wo_ref[...])

    @pl.when(c == pl.num_programs(1) - 1)
    def _():
        sfin_ref[0] = s_scr[...]


def _ret_core(lg, q, k, v, gate, x, s0, wo, n_heads, dk, dv):
    b, t, d = x.shape
    blk = RET_TILE if t % RET_TILE == 0 else t
    nq, nv = n_heads * dk, n_heads * dv
    tok = lambda width: pl.BlockSpec((1, blk, width), lambda bi, c: (bi, c, 0))
    state = pl.BlockSpec((1, n_heads, dk, dv), lambda bi, c: (bi, 0, 0, 0))
    return pl.pallas_call(
        functools.partial(_ret_core_kernel, n_heads=n_heads, dk=dk, dv=dv, blk=blk),
        out_shape=(jax.ShapeDtypeStruct((b, t, d), F32),
                   jax.ShapeDtypeStruct((b, n_heads, dk, dv), F32)),
        grid=(b, t // blk),
        in_specs=[pl.BlockSpec(memory_space=pltpu.SMEM), tok(nq), tok(nq), tok(nv), tok(nv),
                  tok(d), state, _resident((nv, d))],
        out_specs=(tok(d), state),
        scratch_shapes=[pltpu.VMEM((n_heads, dk, dv), F32), pltpu.VMEM((blk, nv), BF16)],
        compiler_params=_params("parallel", "arbitrary"),
        name="ret_core",
    )(lg, q, k, v, gate, x, s0, wo)


def _lambda_init(layer_idx):
    return 0.8 - 0.6 * math.exp(-0.3 * layer_idx)


def kernel(x_prompt, x_sample, cache_diff_k, cache_diff_v, state_ret, ffn1_norm, ffn1_w_gate, ffn1_w_up, ffn1_w_down, mix_norm, da_w_qkv, da_lambda_q1, da_lambda_k1, da_lambda_q2, da_lambda_k2, da_subln, da_w_o, ret_w_in, ret_w_o, ffn2_norm, ffn2_w_gate, ffn2_w_up, ffn2_w_down, final_norm):
    bp, tp, d = x_prompt.shape
    bs, ts, _ = x_sample.shape
    past = cache_diff_k.shape[2]
    da_heads = cache_diff_k.shape[3]
    dh = cache_diff_k.shape[5]
    ret_heads, dk, dv = state_ret.shape[2:]
    depth = ffn1_norm.shape[0]
    bf = lambda w: w.astype(BF16)

    xp = x_prompt.reshape(bp * tp, d)
    xs = x_sample.reshape(bs * ts, d)
    kp_list, vp_list, sp_list = [], [], []
    ks_list, vs_list, ss_list = [], [], []
    for i in range(depth):
        w1 = (ffn1_norm[i], bf(ffn1_w_gate[i]), bf(ffn1_w_up[i]), bf(ffn1_w_down[i]))
        xp = _ffn(xp, *w1)
        xs = _ffn(xs, *w1)
        if i % 2 == 0:
            a = i // 2
            lam_init = _lambda_init(i)
            w_qkv, w_o = bf(da_w_qkv[a]), bf(da_w_o[a])
            lams = [v[a].reshape(1, dh) for v in (da_lambda_q1, da_lambda_k1, da_lambda_q2, da_lambda_k2)]
            w_vt = bf(da_w_qkv[a][:, 2 * d:].T)
            qp, kp, vp, kpb, vpt = _da_proj(xp, mix_norm[i], w_qkv, w_vt, dh ** -0.5, True)
            qs, ks, vs, ksb, vsb = _da_proj(xs, mix_norm[i], w_qkv, w_vt, dh ** -0.5, False)
            seq = lambda z, b, t: z.reshape(b, t, d)
            xp = _da_attn_prompt(lams, seq(qp, bp, tp), seq(kpb, bp, tp), vpt,
                                 seq(xp, bp, tp), w_o, da_subln[a], da_heads,
                                 lam_init).reshape(bp * tp, d)
            xs = _da_attn_sample(lams, seq(qs, bs, ts), cache_diff_k[a].reshape(bs, past, d),
                                 cache_diff_v[a].reshape(bs, past, d), seq(ksb, bs, ts),
                                 seq(vsb, bs, ts), seq(xs, bs, ts), w_o, da_subln[a], da_heads,
                                 lam_init).reshape(bs * ts, d)
            kp_list.append(kp.reshape(bp, tp, da_heads, 2, dh))
            vp_list.append(vp.reshape(bp, tp, da_heads, 2 * dh))
            ks_list.append(ks.reshape(bs, ts, da_heads, 2, dh))
            vs_list.append(vs.reshape(bs, ts, da_heads, 2 * dh))
        else:
            r = i // 2
            w_in, w_o = bf(ret_w_in[r]), bf(ret_w_o[r])
            lg = jnp.log1p(-jnp.exp2(-5.0 - jnp.arange(ret_heads, dtype=F32)))
            nq, nv = ret_heads * dk, ret_heads * dv
            groups = ((xp, bp, tp, jnp.arange(tp, dtype=F32),
                       jnp.zeros((bp, ret_heads, dk, dv), F32), sp_list),
                      (xs, bs, ts, past + jnp.arange(ts, dtype=F32),
                       state_ret[r].astype(F32), ss_list))
            outs = []
            for x, b, t, pos, s0, s_list in groups:
                cos, sin = _rope_tables(pos, dk)
                if t % ROW_TILE != 0:
                    cos, sin = jnp.tile(cos, (b, 1)), jnp.tile(sin, (b, 1))
                q, k, v, gate = _ret_proj(x, mix_norm[i], w_in, cos, sin, ret_heads, dk, dv)
                y, s_fin = _ret_core(lg, q.reshape(b, t, nq), k.reshape(b, t, nq),
                                     v.reshape(b, t, nv), gate.reshape(b, t, nv),
                                     x.reshape(b, t, d), s0, w_o, ret_heads, dk, dv)
                outs.append(y.reshape(b * t, d))
                s_list.append(s_fin)
            xp, xs = outs
        fin = final_norm if i == depth - 1 else None
        w2 = (ffn2_norm[i], bf(ffn2_w_gate[i]), bf(ffn2_w_up[i]), bf(ffn2_w_down[i]))
        xp = _ffn(xp, *w2, final_g=fin)
        xs = _ffn(xs, *w2, final_g=fin)

    return (xp.reshape(bp, tp, d), xs.reshape(bs, ts, d),
            jnp.stack(kp_list), jnp.stack(vp_list), jnp.stack(sp_list),
            jnp.stack(ks_list), jnp.stack(vs_list), jnp.stack(ss_list))
```

```python
import functools
import math

import jax
import jax.numpy as jnp
from jax import lax
from jax.experimental import pallas as pl
from jax.experimental.pallas import tpu as pltpu

F32 = jnp.float32
BF16 = jnp.bfloat16

NORM_EPS = 1e-6
DA_SUBLN_EPS = 1e-5
RET_GN_EPS = 1e-6
ROPE_BASE = 10000.0
CHUNK = 64
NEG_INF = -1e30

LANES = 128
V7X_VMEM_BYTES = 64 * 1024 * 1024
VMEM_LIMIT_BYTES = (V7X_VMEM_BYTES * 3) // 4

ROW_TILE = 512
ATTN_TILE = 512
RET_TILE = 256


def _params(*semantics):
    return pltpu.CompilerParams(dimension_semantics=semantics,
                                vmem_limit_bytes=VMEM_LIMIT_BYTES)


def _resident(shape):
    return pl.BlockSpec(shape, lambda *_: (0,) * len(shape),
                        pipeline_mode=pl.Buffered(1))


def _rmsnorm(x, g, eps):
    return x * lax.rsqrt(jnp.mean(x * x, axis=-1, keepdims=True) + eps) * g


def _dot(a, b):
    return jnp.dot(a, b, preferred_element_type=F32)


def _dot_nt(a, b):
    return lax.dot_general(a, b, (((1,), (1,)), ((), ())), preferred_element_type=F32)


def _dot_tn(a, b):
    return lax.dot_general(a, b, (((0,), (0,)), ((), ())), preferred_element_type=F32)


def _row_tile(n):
    return ROW_TILE if n % ROW_TILE == 0 else n


def _ffn_kernel(x_ref, g_ref, wg_ref, wu_ref, wd_ref, *rest, f_tile, final):
    if final:
        fg_ref, o_ref = rest
    else:
        (o_ref,) = rest
    x = x_ref[...]
    h = _rmsnorm(x, g_ref[...], NORM_EPS).astype(BF16)
    acc = None
    for c in range(wg_ref.shape[1] // f_tile):
        sl = slice(c * f_tile, (c + 1) * f_tile)
        gate = _dot(h, wg_ref[:, sl])
        up = _dot(h, wu_ref[:, sl])
        a = (gate * jax.nn.sigmoid(gate) * up).astype(BF16)
        d = _dot(a, wd_ref[sl, :])
        acc = d if acc is None else acc + d
    y = x + 0.5 * acc
    if final:
        y = _rmsnorm(y, fg_ref[...], NORM_EPS)
    o_ref[...] = y


def _ffn(x, g, wg, wu, wd, final_g=None):
    n, d = x.shape
    f = wg.shape[1]
    tm = _row_tile(n)
    f_tile = 256 if f % 256 == 0 else f
    final = final_g is not None
    row = pl.BlockSpec((tm, d), lambda i: (i, 0))
    in_specs = [row, _resident((1, d)), _resident((d, f)), _resident((d, f)), _resident((f, d))]
    args = [x, g.reshape(1, d), wg, wu, wd]
    if final:
        in_specs.append(_resident((1, d)))
        args.append(final_g.reshape(1, d))
    return pl.pallas_call(
        functools.partial(_ffn_kernel, f_tile=f_tile, final=final),
        out_shape=jax.ShapeDtypeStruct((n, d), F32),
        grid=(n // tm,),
        in_specs=in_specs,
        out_specs=row,
        compiler_params=_params("parallel"),
        name="ffn_final" if final else "ffn",
    )(*args)


def _da_proj_kernel(x_ref, g_ref, w_ref, wvt_ref, q_ref, k_ref, v_ref, kb_ref, vb_ref, *,
                    q_scale, transposed_v, n_heads):
    tm, d = x_ref.shape
    lane_tiles = d // LANES
    h = _rmsnorm(x_ref[...], g_ref[...], NORM_EPS).astype(BF16)
    q_ref[...] = (_dot(h, w_ref[:, 0:d]) * q_scale).astype(BF16)
    k = _dot(h, w_ref[:, d:2 * d])
    kb_ref[...] = k.astype(BF16)
    v = _dot(h, w_ref[:, 2 * d:3 * d])
    for j in range(lane_tiles):
        cols = slice(j * LANES, (j + 1) * LANES)
        k_ref[pl.ds(j, tm, stride=lane_tiles), :] = k[:, cols]
        head, half = divmod(j, lane_tiles // n_heads)
        v_ref[pl.ds(half * n_heads + head, tm, stride=lane_tiles), :] = v[:, cols]
    if transposed_v:
        vb_ref[...] = _dot_nt(wvt_ref[...], h).astype(BF16)
    else:
        vb_ref[...] = v.astype(BF16)


def _da_proj(x, g, w, wvt, q_scale, transposed_v, n_heads):
    n, d = x.shape
    tm = _row_tile(n)
    lane_tiles = d // LANES
    row = pl.BlockSpec((tm, d), lambda i: (i, 0))
    rows = pl.BlockSpec((tm * lane_tiles, LANES), lambda i: (i, 0))
    if transposed_v:
        vb_shape, vb_spec = (d, n), pl.BlockSpec((d, tm), lambda i: (0, i))
    else:
        vb_shape, vb_spec = (n, d), row
    return pl.pallas_call(
        functools.partial(_da_proj_kernel, q_scale=q_scale, transposed_v=transposed_v,
                          n_heads=n_heads),
        out_shape=(jax.ShapeDtypeStruct((n, d), BF16),
                   jax.ShapeDtypeStruct((n * lane_tiles, LANES), F32),
                   jax.ShapeDtypeStruct((n * lane_tiles, LANES), F32),
                   jax.ShapeDtypeStruct((n, d), BF16),
                   jax.ShapeDtypeStruct(vb_shape, BF16)),
        grid=(n // tm,),
        in_specs=[row, _resident((1, d)), _resident((d, 3 * d)), _resident((d, d))],
        out_specs=(row, rows, rows, row, vb_spec),
        compiler_params=_params("parallel"),
        name="da_proj_t" if transposed_v else "da_proj",
    )(x, g.reshape(1, d), w, wvt)


def _key_rows_to_cache(k_rows, b, t, n_heads, dh):
    return k_rows.reshape(b, t, n_heads, 2, dh)


def _value_rows_to_cache(v_rows, b, t, n_heads, dh):
    v = v_rows.reshape(b, t, 2, n_heads, dh)
    return jnp.swapaxes(v, 2, 3).reshape(b, t, n_heads, 2 * dh)


def _key_cache_to_rows(k_cache):
    b, t, n_heads, _, dh = k_cache.shape
    return k_cache.reshape(b, t * n_heads * 2, dh)


def _value_cache_to_rows(v_cache):
    b, t, n_heads, dv = v_cache.shape
    v = v_cache.reshape(b, t, n_heads, 2, dv // 2)
    return jnp.swapaxes(v, 2, 3).reshape(b, t * 2 * n_heads, dv // 2)


def _da_lambda(lq1_ref, lk1_ref, lq2_ref, lk2_ref, lam_init):
    s1 = jnp.sum(lq1_ref[...] * lk1_ref[...], axis=1, keepdims=True)
    s2 = jnp.sum(lq2_ref[...] * lk2_ref[...], axis=1, keepdims=True)
    return jnp.exp(s1) - jnp.exp(s2) + lam_init


def _da_head_out(o, sub_ref, lam_init):
    return (_rmsnorm(o, sub_ref[...], DA_SUBLN_EPS) * (1.0 - lam_init)).astype(BF16)


def _da_attn_kernel(lq1_ref, lk1_ref, lq2_ref, lk2_ref, q_ref, k_ref, vt_ref, x_ref, wo_ref,
                    sub_ref, o_ref, acc_ref, cat_ref, *, n_heads, dh, tq, lam_init):
    i = pl.program_id(1)
    lam = _da_lambda(lq1_ref, lk1_ref, lq2_ref, lk2_ref, lam_init)
    key_chunk = lax.broadcasted_iota(jnp.int32, (tq, tq), 0) // CHUNK
    qry_chunk = lax.broadcasted_iota(jnp.int32, (tq, tq), 1) // CHUNK
    diag_mask = key_chunk <= qry_chunk

    for h in range(n_heads):
        c0 = h * 2 * dh
        qs = (q_ref[0, :, c0:c0 + dh], q_ref[0, :, c0 + dh:c0 + 2 * dh])
        acc_ref[...] = jnp.zeros_like(acc_ref)

        def scores(j, c0=c0, qs=qs):
            ks = pl.multiple_of(j * tq, tq)
            return tuple(_dot_nt(k_ref[0, pl.ds(ks, tq), c0 + c * dh:c0 + (c + 1) * dh], qs[c])
                         for c in range(2))

        def absorb(j, stats, s_pair, masked, c0=c0):
            ks = pl.multiple_of(j * tq, tq)
            vt = vt_ref[c0:c0 + 2 * dh, pl.ds(ks, tq)]
            new = []
            for c in range(2):
                m, l = stats[2 * c], stats[2 * c + 1]
                s = jnp.where(diag_mask, s_pair[c], NEG_INF) if masked else s_pair[c]
                m_new = jnp.maximum(m, jnp.max(s, axis=0, keepdims=True))
                alpha = jnp.exp(m - m_new)
                p = jnp.exp(s - m_new)
                acc_ref[c] = alpha * acc_ref[c] + _dot(vt, p.astype(BF16))
                new += [m_new, alpha * l + jnp.sum(p, axis=0, keepdims=True)]
            return tuple(new)

        def step(j, carry):
            stats, s_pair = carry
            s_next = scores(j + 1)
            return absorb(j, stats, s_pair, masked=False), s_next

        m0 = jnp.full((1, tq), NEG_INF, F32)
        l0 = jnp.zeros((1, tq), F32)
        stats, s_diag = lax.fori_loop(0, i, step, ((m0, l0, m0, l0), scores(0)))
        _, l1, _, l2 = absorb(i, stats, s_diag, masked=True)
        o = acc_ref[0] / l1 - lam * (acc_ref[1] / l2)
        on = o * lax.rsqrt(jnp.mean(o * o, axis=0, keepdims=True) + DA_SUBLN_EPS) * sub_ref[...]
        cat_ref[c0:c0 + 2 * dh, :] = (on * (1.0 - lam_init)).astype(BF16)

    o_ref[0] = x_ref[0] + _dot_tn(cat_ref[...], wo_ref[...])


def _da_attn_prompt(lams, q, kb, vt, x, wo, subln, n_heads, lam_init):
    b, t, d = x.shape
    dh = d // (2 * n_heads)
    tq = ATTN_TILE if t % ATTN_TILE == 0 else t
    assert tq % CHUNK == 0
    blk = pl.BlockSpec((1, tq, d), lambda bi, i: (bi, i, 0))
    seq = pl.BlockSpec((1, t, d), lambda bi, i: (bi, 0, 0), pipeline_mode=pl.Buffered(1))
    seq_t = pl.BlockSpec((d, t), lambda bi, i: (0, bi), pipeline_mode=pl.Buffered(1))
    lam_spec = _resident((1, dh))
    return pl.pallas_call(
        functools.partial(_da_attn_kernel, n_heads=n_heads, dh=dh, tq=tq, lam_init=lam_init),
        out_shape=jax.ShapeDtypeStruct((b, t, d), F32),
        grid=(b, t // tq),
        in_specs=[lam_spec] * 4 + [blk, seq, seq_t, blk, _resident((d, d)),
                                   _resident((2 * dh, 1))],
        out_specs=blk,
        scratch_shapes=[pltpu.VMEM((2, 2 * dh, tq), F32), pltpu.VMEM((d, tq), BF16)],
        compiler_params=_params("parallel", "arbitrary"),
        name="da_attn_prompt",
    )(*lams, q, kb, vt, x, wo, subln.reshape(2 * dh, 1))


def _da_attn_sample_kernel(lq1_ref, lk1_ref, lq2_ref, lk2_ref, q_ref, kc_ref, vc_ref, kn_ref,
                           vn_ref, x_ref, wo_ref, sub_ref, o_ref, cat_ref, *, n_heads, dh,
                           lam_init):
    lam = _da_lambda(lq1_ref, lk1_ref, lq2_ref, lk2_ref, lam_init)
    rows_per_token = 2 * n_heads
    past = kc_ref.shape[1] // rows_per_token

    def cached(ref, row):
        return ref.at[0][pl.ds(row, past, stride=rows_per_token), :].astype(BF16)

    for h in range(n_heads):
        c0 = h * 2 * dh
        probs = []
        for c in range(2):
            qh = q_ref[0, :, c0 + c * dh:c0 + (c + 1) * dh]
            sc = _dot_nt(qh, cached(kc_ref, 2 * h + c))
            sn = _dot_nt(qh, kn_ref[0, :, c0 + c * dh:c0 + (c + 1) * dh])
            m = jnp.maximum(jnp.max(sc, axis=1, keepdims=True), jnp.max(sn, axis=1, keepdims=True))
            pc = jnp.exp(sc - m)
            pn = jnp.exp(sn - m)
            l = jnp.sum(pc, axis=1, keepdims=True) + jnp.sum(pn, axis=1, keepdims=True)
            probs.append((pc / l, pn / l))
        ac = (probs[0][0] - lam * probs[1][0]).astype(BF16)
        an = (probs[0][1] - lam * probs[1][1]).astype(BF16)
        vc = jnp.concatenate([cached(vc_ref, h), cached(vc_ref, n_heads + h)], axis=1)
        o = _dot(ac, vc) + _dot(an, vn_ref[0, :, c0:c0 + 2 * dh])
        cat_ref[:, c0:c0 + 2 * dh] = _da_head_out(o, sub_ref, lam_init)
    o_ref[0] = x_ref[0] + _dot(cat_ref[...], wo_ref[...])


def _da_attn_sample(lams, q, k_rows, v_rows, kb, vb, x, wo, subln, n_heads, lam_init):
    b, ts, d = x.shape
    dh = d // (2 * n_heads)
    past = k_rows.shape[1] // (2 * n_heads)
    assert past % CHUNK == 0 and ts <= CHUNK
    new = pl.BlockSpec((1, ts, d), lambda bi: (bi, 0, 0))
    cache = pl.BlockSpec((1, past * 2 * n_heads, dh), lambda bi: (bi, 0, 0))
    lam_spec = _resident((1, dh))
    return pl.pallas_call(
        functools.partial(_da_attn_sample_kernel, n_heads=n_heads, dh=dh, lam_init=lam_init),
        out_shape=jax.ShapeDtypeStruct((b, ts, d), F32),
        grid=(b,),
        in_specs=[lam_spec] * 4 + [new, cache, cache, new, new, new, _resident((d, d)),
                                   _resident((1, 2 * dh))],
        out_specs=new,
        scratch_shapes=[pltpu.VMEM((ts, d), BF16)],
        compiler_params=_params("parallel"),
        name="da_attn_sample",
    )(*lams, q, k_rows, v_rows, kb, vb, x, wo, subln.reshape(1, 2 * dh))


def _rope_tables(pos, dk):
    angle = 1.0 / (ROPE_BASE ** jnp.linspace(0.0, 1.0, dk // 2, dtype=F32))
    angle = jnp.repeat(angle, 2)
    theta = pos[:, None] * angle[None, :]
    sign = jnp.where(jnp.arange(dk) % 2 == 0, -1.0, 1.0).astype(F32)
    return jnp.cos(theta), jnp.sin(theta) * sign[None, :]


def _ret_proj_kernel(x_ref, g_ref, w_ref, cos_ref, sin_ref, q_ref, k_ref, v_ref, gate_ref, *,
                     n_heads, dk, k_scale):
    nq = n_heads * dk
    nv = v_ref.shape[1]
    h = _rmsnorm(x_ref[...], g_ref[...], NORM_EPS).astype(BF16)
    cos = cos_ref[...]
    sin = sin_ref[...]
    even = lax.broadcasted_iota(jnp.int32, cos.shape, 1) % 2 == 0

    def rotary(a):
        rot = jnp.where(even, pltpu.roll(a, dk - 1, 1), pltpu.roll(a, 1, 1))
        return a * cos + rot * sin

    q = _dot(h, w_ref[:, 0:nq])
    k = _dot(h, w_ref[:, nq:2 * nq])
    for hd in range(n_heads):
        sl = slice(hd * dk, (hd + 1) * dk)
        q_ref[:, sl] = rotary(q[:, sl]).astype(BF16)
        k_ref[:, sl] = (rotary(k[:, sl]) * k_scale).astype(BF16)
    v_ref[...] = _dot(h, w_ref[:, 2 * nq:2 * nq + nv]).astype(BF16)
    gate_ref[...] = _dot(h, w_ref[:, 2 * nq + nv:2 * nq + 2 * nv])


def _ret_proj(x, g, w, cos, sin, n_heads, dk, dv):
    n, d = x.shape
    nq, nv = n_heads * dk, n_heads * dv
    tm = _row_tile(cos.shape[0])
    pos_blocks = cos.shape[0] // tm
    row = lambda width: pl.BlockSpec((tm, width), lambda i: (i, 0))
    tab = pl.BlockSpec((tm, dk), lambda i: (i % pos_blocks, 0))
    return pl.pallas_call(
        functools.partial(_ret_proj_kernel, n_heads=n_heads, dk=dk, k_scale=dk ** -0.5),
        out_shape=(jax.ShapeDtypeStruct((n, nq), BF16), jax.ShapeDtypeStruct((n, nq), BF16),
                   jax.ShapeDtypeStruct((n, nv), BF16), jax.ShapeDtypeStruct((n, nv), F32)),
        grid=(n // tm,),
        in_specs=[row(d), _resident((1, d)), _resident((d, 2 * nq + 2 * nv)), tab, tab],
        out_specs=(row(nq), row(nq), row(nv), row(nv)),
        compiler_params=_params("parallel"),
        name="ret_proj",
    )(x, g.reshape(1, d), w, cos, sin)


def _ret_core_kernel(lg_ref, q_ref, k_ref, v_ref, gate_ref, x_ref, s0_ref, wo_ref, o_ref,
                     sfin_ref, s_scr, y_scr, *, n_heads, dk, dv, blk):
    c = pl.program_id(1)

    @pl.when(c == 0)
    def _():
        s_scr[...] = s0_ref[0]

    row = lax.broadcasted_iota(jnp.int32, (blk, blk), 0)
    col = lax.broadcasted_iota(jnp.int32, (blk, blk), 1)
    rel = (row - col).astype(F32)
    idx = lax.broadcasted_iota(jnp.int32, (blk, 1), 0).astype(F32)

    for h in range(n_heads):
        lg = lg_ref[h]
        dmask = jnp.where(rel >= 0, jnp.exp(jnp.maximum(rel, 0.0) * lg), 0.0)
        qh = q_ref[0, :, h * dk:(h + 1) * dk]
        kh = k_ref[0, :, h * dk:(h + 1) * dk]
        vh = v_ref[0, :, h * dv:(h + 1) * dv]
        s = s_scr[h]
        qk = _dot_nt(qh, kh) * dmask
        inner = _dot(qk.astype(BF16), vh)
        cross = _dot(qh, s.astype(BF16)) * jnp.exp((idx + 1.0) * lg)
        o = inner + cross
        kd = (kh.astype(F32) * jnp.exp((blk - 1.0 - idx) * lg)).astype(BF16)
        s_decay = jnp.exp(jnp.full((1, dv), float(blk), F32) * lg)
        s_scr[h] = s * s_decay + _dot_tn(kd, vh)
        on = o * lax.rsqrt(jnp.mean(o * o, axis=-1, keepdims=True) + RET_GN_EPS)
        gate = gate_ref[0, :, h * dv:(h + 1) * dv]
        y_scr[:, h * dv:(h + 1) * dv] = (gate * jax.nn.sigmoid(gate) * on).astype(BF16)

    o_ref[0] = x_ref[0] + _dot(y_scr[...], wo_ref[...])

    @pl.when(c == pl.num_programs(1) - 1)
    def _():
        sfin_ref[0] = s_scr[...]


def _ret_core(lg, q, k, v, gate, x, s0, wo, n_heads, dk, dv):
    b, t, d = x.shape
    blk = RET_TILE if t % RET_TILE == 0 else t
    nq, nv = n_heads * dk, n_heads * dv
    tok = lambda width: pl.BlockSpec((1, blk, width), lambda bi, c: (bi, c, 0))
    state = pl.BlockSpec((1, n_heads, dk, dv), lambda bi, c: (bi, 0, 0, 0))
    return pl.pallas_call(
        functools.partial(_ret_core_kernel, n_heads=n_heads, dk=dk, dv=dv, blk=blk),
        out_shape=(jax.ShapeDtypeStruct((b, t, d), F32),
                   jax.ShapeDtypeStruct((b, n_heads, dk, dv), F32)),
        grid=(b, t // blk),
        in_specs=[pl.BlockSpec(memory_space=pltpu.SMEM), tok(nq), tok(nq), tok(nv), tok(nv),
                  tok(d), state, _resident((nv, d))],
        out_specs=(tok(d), state),
        scratch_shapes=[pltpu.VMEM((n_heads, dk, dv), F32), pltpu.VMEM((blk, nv), BF16)],
        compiler_params=_params("parallel", "arbitrary"),
        name="ret_core",
    )(lg, q, k, v, gate, x, s0, wo)


def _lambda_init(layer_idx):
    return 0.8 - 0.6 * math.exp(-0.3 * layer_idx)


def kernel(x_prompt, x_sample, cache_diff_k, cache_diff_v, state_ret, ffn1_norm, ffn1_w_gate, ffn1_w_up, ffn1_w_down, mix_norm, da_w_qkv, da_lambda_q1, da_lambda_k1, da_lambda_q2, da_lambda_k2, da_subln, da_w_o, ret_w_in, ret_w_o, ffn2_norm, ffn2_w_gate, ffn2_w_up, ffn2_w_down, final_norm):
    bp, tp, d = x_prompt.shape
    bs, ts, _ = x_sample.shape
    past = cache_diff_k.shape[2]
    da_heads = cache_diff_k.shape[3]
    dh = cache_diff_k.shape[5]
    ret_heads, dk, dv = state_ret.shape[2:]
    depth = ffn1_norm.shape[0]
    bf = lambda w: w.astype(BF16)

    xp = x_prompt.reshape(bp * tp, d)
    xs = x_sample.reshape(bs * ts, d)
    kp_list, vp_list, sp_list = [], [], []
    ks_list, vs_list, ss_list = [], [], []
    for i in range(depth):
        w1 = (ffn1_norm[i], bf(ffn1_w_gate[i]), bf(ffn1_w_up[i]), bf(ffn1_w_down[i]))
        xp = _ffn(xp, *w1)
        xs = _ffn(xs, *w1)
        if i % 2 == 0:
            a = i // 2
            lam_init = _lambda_init(i)
            w_qkv, w_o = bf(da_w_qkv[a]), bf(da_w_o[a])
            lams = [v[a].reshape(1, dh) for v in (da_lambda_q1, da_lambda_k1, da_lambda_q2, da_lambda_k2)]
            w_vt = bf(da_w_qkv[a][:, 2 * d:].T)
            qp, kp, vp, kpb, vpt = _da_proj(xp, mix_norm[i], w_qkv, w_vt, dh ** -0.5, True, da_heads)
            qs, ks, vs, ksb, vsb = _da_proj(xs, mix_norm[i], w_qkv, w_vt, dh ** -0.5, False, da_heads)
            seq = lambda z, b, t: z.reshape(b, t, d)
            xp = _da_attn_prompt(lams, seq(qp, bp, tp), seq(kpb, bp, tp), vpt,
                                 seq(xp, bp, tp), w_o, da_subln[a], da_heads,
                                 lam_init).reshape(bp * tp, d)
            xs = _da_attn_sample(lams, seq(qs, bs, ts), _key_cache_to_rows(cache_diff_k[a]),
                                 _value_cache_to_rows(cache_diff_v[a]), seq(ksb, bs, ts),
                                 seq(vsb, bs, ts), seq(xs, bs, ts), w_o, da_subln[a], da_heads,
                                 lam_init).reshape(bs * ts, d)
            kp_list.append(_key_rows_to_cache(kp, bp, tp, da_heads, dh))
            vp_list.append(_value_rows_to_cache(vp, bp, tp, da_heads, dh))
            ks_list.append(_key_rows_to_cache(ks, bs, ts, da_heads, dh))
            vs_list.append(_value_rows_to_cache(vs, bs, ts, da_heads, dh))
        else:
            r = i // 2
            w_in, w_o = bf(ret_w_in[r]), bf(ret_w_o[r])
            lg = jnp.log1p(-jnp.exp2(-5.0 - jnp.arange(ret_heads, dtype=F32)))
            nq, nv = ret_heads * dk, ret_heads * dv
            groups = ((xp, bp, tp, jnp.arange(tp, dtype=F32),
                       jnp.zeros((bp, ret_heads, dk, dv), F32), sp_list),
                      (xs, bs, ts, past + jnp.arange(ts, dtype=F32),
                       state_ret[r].astype(F32), ss_list))
            outs = []
            for x, b, t, pos, s0, s_list in groups:
                cos, sin = _rope_tables(pos, dk)
                if t % ROW_TILE != 0:
                    cos, sin = jnp.tile(cos, (b, 1)), jnp.tile(sin, (b, 1))
                q, k, v, gate = _ret_proj(x, mix_norm[i], w_in, cos, sin, ret_heads, dk, dv)
                y, s_fin = _ret_core(lg, q.reshape(b, t, nq), k.reshape(b, t, nq),
                                     v.reshape(b, t, nv), gate.reshape(b, t, nv),
                                     x.reshape(b, t, d), s0, w_o, ret_heads, dk, dv)
                outs.append(y.reshape(b * t, d))
                s_list.append(s_fin)
            xp, xs = outs
        fin = final_norm if i == depth - 1 else None
        w2 = (ffn2_norm[i], bf(ffn2_w_gate[i]), bf(ffn2_w_up[i]), bf(ffn2_w_down[i]))
        xp = _ffn(xp, *w2, final_g=fin)
        xs = _ffn(xs, *w2, final_g=fin)

    return (xp.reshape(bp, tp, d), xs.reshape(bs, ts, d),
            jnp.stack(kp_list), jnp.stack(vp_list), jnp.stack(sp_list),
            jnp.stack(ks_list), jnp.stack(vs_list), jnp.stack(ss_list))
```

```python
import functools
import math

import jax
import jax.numpy as jnp
from jax import lax
from jax.experimental import pallas as pl
from jax.experimental.pallas import tpu as pltpu

F32 = jnp.float32
BF16 = jnp.bfloat16

NORM_EPS = 1e-6
DA_SUBLN_EPS = 1e-5
RET_GN_EPS = 1e-6
ROPE_BASE = 10000.0
CHUNK = 64
NEG_INF = -1e30

LANES = 128
V7X_VMEM_BYTES = 64 * 1024 * 1024
VMEM_LIMIT_BYTES = (V7X_VMEM_BYTES * 3) // 4

ROW_TILE = 512
ATTN_TILE = 512
RET_TILE = 256
SCORE_LOOKAHEAD = 2


def _params(*semantics):
    return pltpu.CompilerParams(dimension_semantics=semantics,
                                vmem_limit_bytes=VMEM_LIMIT_BYTES)


def _resident(shape):
    return pl.BlockSpec(shape, lambda *_: (0,) * len(shape),
                        pipeline_mode=pl.Buffered(1))


def _rmsnorm(x, g, eps):
    return x * lax.rsqrt(jnp.mean(x * x, axis=-1, keepdims=True) + eps) * g


def _dot(a, b):
    return jnp.dot(a, b, preferred_element_type=F32)


def _dot_nt(a, b):
    return lax.dot_general(a, b, (((1,), (1,)), ((), ())), preferred_element_type=F32)


def _dot_tn(a, b):
    return lax.dot_general(a, b, (((0,), (0,)), ((), ())), preferred_element_type=F32)


def _row_tile(n):
    return ROW_TILE if n % ROW_TILE == 0 else n


def _ffn_kernel(x_ref, g_ref, wg_ref, wu_ref, wd_ref, *rest, f_tile, final):
    if final:
        fg_ref, o_ref = rest
    else:
        (o_ref,) = rest
    x = x_ref[...]
    h = _rmsnorm(x, g_ref[...], NORM_EPS).astype(BF16)
    acc = None
    for c in range(wg_ref.shape[1] // f_tile):
        sl = slice(c * f_tile, (c + 1) * f_tile)
        gate = _dot(h, wg_ref[:, sl])
        up = _dot(h, wu_ref[:, sl])
        a = (gate * jax.nn.sigmoid(gate) * up).astype(BF16)
        d = _dot(a, wd_ref[sl, :])
        acc = d if acc is None else acc + d
    y = x + 0.5 * acc
    if final:
        y = _rmsnorm(y, fg_ref[...], NORM_EPS)
    o_ref[...] = y


def _ffn(x, g, wg, wu, wd, final_g=None):
    n, d = x.shape
    f = wg.shape[1]
    tm = _row_tile(n)
    f_tile = 256 if f % 256 == 0 else f
    final = final_g is not None
    row = pl.BlockSpec((tm, d), lambda i: (i, 0))
    in_specs = [row, _resident((1, d)), _resident((d, f)), _resident((d, f)), _resident((f, d))]
    args = [x, g.reshape(1, d), wg, wu, wd]
    if final:
        in_specs.append(_resident((1, d)))
        args.append(final_g.reshape(1, d))
    return pl.pallas_call(
        functools.partial(_ffn_kernel, f_tile=f_tile, final=final),
        out_shape=jax.ShapeDtypeStruct((n, d), F32),
        grid=(n // tm,),
        in_specs=in_specs,
        out_specs=row,
        compiler_params=_params("parallel"),
        name="ffn_final" if final else "ffn",
    )(*args)


def _da_proj_kernel(x_ref, g_ref, w_ref, wvt_ref, q_ref, k_ref, v_ref, kb_ref, vb_ref, *,
                    q_scale, transposed_v, n_heads):
    tm, d = x_ref.shape
    lane_tiles = d // LANES
    h = _rmsnorm(x_ref[...], g_ref[...], NORM_EPS).astype(BF16)
    q_ref[...] = (_dot(h, w_ref[:, 0:d]) * q_scale).astype(BF16)
    k = _dot(h, w_ref[:, d:2 * d])
    kb_ref[...] = k.astype(BF16)
    v = _dot(h, w_ref[:, 2 * d:3 * d])
    for j in range(lane_tiles):
        cols = slice(j * LANES, (j + 1) * LANES)
        k_ref[pl.ds(j, tm, stride=lane_tiles), :] = k[:, cols]
        head, half = divmod(j, lane_tiles // n_heads)
        v_ref[pl.ds(half * n_heads + head, tm, stride=lane_tiles), :] = v[:, cols]
    if transposed_v:
        vb_ref[...] = _dot_nt(wvt_ref[...], h).astype(BF16)
    else:
        vb_ref[...] = v.astype(BF16)


def _da_proj(x, g, w, wvt, q_scale, transposed_v, n_heads):
    n, d = x.shape
    tm = _row_tile(n)
    lane_tiles = d // LANES
    row = pl.BlockSpec((tm, d), lambda i: (i, 0))
    rows = pl.BlockSpec((tm * lane_tiles, LANES), lambda i: (i, 0))
    if transposed_v:
        vb_shape, vb_spec = (d, n), pl.BlockSpec((d, tm), lambda i: (0, i))
    else:
        vb_shape, vb_spec = (n, d), row
    return pl.pallas_call(
        functools.partial(_da_proj_kernel, q_scale=q_scale, transposed_v=transposed_v,
                          n_heads=n_heads),
        out_shape=(jax.ShapeDtypeStruct((n, d), BF16),
                   jax.ShapeDtypeStruct((n * lane_tiles, LANES), F32),
                   jax.ShapeDtypeStruct((n * lane_tiles, LANES), F32),
                   jax.ShapeDtypeStruct((n, d), BF16),
                   jax.ShapeDtypeStruct(vb_shape, BF16)),
        grid=(n // tm,),
        in_specs=[row, _resident((1, d)), _resident((d, 3 * d)), _resident((d, d))],
        out_specs=(row, rows, rows, row, vb_spec),
        compiler_params=_params("parallel"),
        name="da_proj_t" if transposed_v else "da_proj",
    )(x, g.reshape(1, d), w, wvt)


def _key_rows_to_cache(k_rows, b, t, n_heads, dh):
    return k_rows.reshape(b, t, n_heads, 2, dh)


def _value_rows_to_cache(v_rows, b, t, n_heads, dh):
    v = v_rows.reshape(b, t, 2, n_heads, dh)
    return jnp.swapaxes(v, 2, 3).reshape(b, t, n_heads, 2 * dh)


def _key_cache_to_rows(k_cache):
    b, t, n_heads, _, dh = k_cache.shape
    return k_cache.reshape(b, t * n_heads * 2, dh)


def _value_cache_to_rows(v_cache):
    b, t, n_heads, dv = v_cache.shape
    v = v_cache.reshape(b, t, n_heads, 2, dv // 2)
    return jnp.swapaxes(v, 2, 3).reshape(b, t * 2 * n_heads, dv // 2)


def _da_lambda(lq1_ref, lk1_ref, lq2_ref, lk2_ref, lam_init):
    s1 = jnp.sum(lq1_ref[...] * lk1_ref[...], axis=1, keepdims=True)
    s2 = jnp.sum(lq2_ref[...] * lk2_ref[...], axis=1, keepdims=True)
    return jnp.exp(s1) - jnp.exp(s2) + lam_init


def _da_head_out(o, sub_ref, lam_init):
    return (_rmsnorm(o, sub_ref[...], DA_SUBLN_EPS) * (1.0 - lam_init)).astype(BF16)


def _da_attn_kernel(lq1_ref, lk1_ref, lq2_ref, lk2_ref, q_ref, k_ref, vt_ref, x_ref, wo_ref,
                    sub_ref, o_ref, acc_ref, cat_ref, *, n_heads, dh, tq, lam_init):
    i = pl.program_id(1)
    lam = _da_lambda(lq1_ref, lk1_ref, lq2_ref, lk2_ref, lam_init)
    key_chunk = lax.broadcasted_iota(jnp.int32, (tq, tq), 0) // CHUNK
    qry_chunk = lax.broadcasted_iota(jnp.int32, (tq, tq), 1) // CHUNK
    diag_mask = key_chunk <= qry_chunk

    acc_ref[...] = jnp.zeros_like(acc_ref)

    def absorb(j, stats, masked):
        ks = pl.multiple_of(j * tq, tq)
        new = []
        n_chains = 2 * n_heads

        def scores(chain):
            cols = slice(chain * dh, (chain + 1) * dh)
            return _dot_nt(k_ref[0, pl.ds(ks, tq), cols], q_ref[0, :, cols])

        pending = [scores(c) for c in range(SCORE_LOOKAHEAD)]
        for chain in range(n_chains):
            h = chain // 2
            m, l = stats[2 * chain], stats[2 * chain + 1]
            s = pending.pop(0)
            if chain + SCORE_LOOKAHEAD < n_chains:
                pending.append(scores(chain + SCORE_LOOKAHEAD))
            if masked:
                s = jnp.where(diag_mask, s, NEG_INF)
            m_new = jnp.maximum(m, jnp.max(s, axis=0, keepdims=True))
            alpha = jnp.exp(m - m_new)
            p = jnp.exp(s - m_new)
            vt = vt_ref[h * 2 * dh:(h + 1) * 2 * dh, pl.ds(ks, tq)]
            acc_ref[chain] = alpha * acc_ref[chain] + _dot(vt, p.astype(BF16))
            new += [m_new, alpha * l + jnp.sum(p, axis=0, keepdims=True)]
        return tuple(new)

    m0 = jnp.full((1, tq), NEG_INF, F32)
    l0 = jnp.zeros((1, tq), F32)
    stats = lax.fori_loop(0, i, functools.partial(absorb, masked=False), (m0, l0) * (2 * n_heads))
    stats = absorb(i, stats, masked=True)

    for h in range(n_heads):
        l1, l2 = stats[4 * h + 1], stats[4 * h + 3]
        o = acc_ref[2 * h] / l1 - lam * (acc_ref[2 * h + 1] / l2)
        on = o * lax.rsqrt(jnp.mean(o * o, axis=0, keepdims=True) + DA_SUBLN_EPS) * sub_ref[...]
        cat_ref[h * 2 * dh:(h + 1) * 2 * dh, :] = (on * (1.0 - lam_init)).astype(BF16)

    o_ref[0] = x_ref[0] + _dot_tn(cat_ref[...], wo_ref[...])


def _da_attn_prompt(lams, q, kb, vt, x, wo, subln, n_heads, lam_init):
    b, t, d = x.shape
    dh = d // (2 * n_heads)
    tq = ATTN_TILE if t % ATTN_TILE == 0 else t
    assert tq % CHUNK == 0
    blk = pl.BlockSpec((1, tq, d), lambda bi, i: (bi, i, 0))
    seq = pl.BlockSpec((1, t, d), lambda bi, i: (bi, 0, 0), pipeline_mode=pl.Buffered(1))
    seq_t = pl.BlockSpec((d, t), lambda bi, i: (0, bi), pipeline_mode=pl.Buffered(1))
    lam_spec = _resident((1, dh))
    return pl.pallas_call(
        functools.partial(_da_attn_kernel, n_heads=n_heads, dh=dh, tq=tq, lam_init=lam_init),
        out_shape=jax.ShapeDtypeStruct((b, t, d), F32),
        grid=(b, t // tq),
        in_specs=[lam_spec] * 4 + [blk, seq, seq_t, blk, _resident((d, d)),
                                   _resident((2 * dh, 1))],
        out_specs=blk,
        scratch_shapes=[pltpu.VMEM((2 * n_heads, 2 * dh, tq), F32), pltpu.VMEM((d, tq), BF16)],
        compiler_params=_params("parallel", "arbitrary"),
        name="da_attn_prompt",
    )(*lams, q, kb, vt, x, wo, subln.reshape(2 * dh, 1))


def _da_attn_sample_kernel(lq1_ref, lk1_ref, lq2_ref, lk2_ref, q_ref, kc_ref, vc_ref, kn_ref,
                           vn_ref, x_ref, wo_ref, sub_ref, o_ref, cat_ref, *, n_heads, dh,
                           lam_init):
    lam = _da_lambda(lq1_ref, lk1_ref, lq2_ref, lk2_ref, lam_init)
    rows_per_token = 2 * n_heads
    past = kc_ref.shape[1] // rows_per_token

    def cached(ref, row):
        return ref.at[0][pl.ds(row, past, stride=rows_per_token), :].astype(BF16)

    for h in range(n_heads):
        c0 = h * 2 * dh
        probs = []
        for c in range(2):
            qh = q_ref[0, :, c0 + c * dh:c0 + (c + 1) * dh]
            sc = _dot_nt(qh, cached(kc_ref, 2 * h + c))
            sn = _dot_nt(qh, kn_ref[0, :, c0 + c * dh:c0 + (c + 1) * dh])
            m = jnp.maximum(jnp.max(sc, axis=1, keepdims=True), jnp.max(sn, axis=1, keepdims=True))
            pc = jnp.exp(sc - m)
            pn = jnp.exp(sn - m)
            l = jnp.sum(pc, axis=1, keepdims=True) + jnp.sum(pn, axis=1, keepdims=True)
            probs.append((pc / l, pn / l))
        ac = (probs[0][0] - lam * probs[1][0]).astype(BF16)
        an = (probs[0][1] - lam * probs[1][1]).astype(BF16)
        vc = jnp.concatenate([cached(vc_ref, h), cached(vc_ref, n_heads + h)], axis=1)
        o = _dot(ac, vc) + _dot(an, vn_ref[0, :, c0:c0 + 2 * dh])
        cat_ref[:, c0:c0 + 2 * dh] = _da_head_out(o, sub_ref, lam_init)
    o_ref[0] = x_ref[0] + _dot(cat_ref[...], wo_ref[...])


def _da_attn_sample(lams, q, k_rows, v_rows, kb, vb, x, wo, subln, n_heads, lam_init):
    b, ts, d = x.shape
    dh = d // (2 * n_heads)
    past = k_rows.shape[1] // (2 * n_heads)
    assert past % CHUNK == 0 and ts <= CHUNK
    new = pl.BlockSpec((1, ts, d), lambda bi: (bi, 0, 0))
    cache = pl.BlockSpec((1, past * 2 * n_heads, dh), lambda bi: (bi, 0, 0))
    lam_spec = _resident((1, dh))
    return pl.pallas_call(
        functools.partial(_da_attn_sample_kernel, n_heads=n_heads, dh=dh, lam_init=lam_init),
        out_shape=jax.ShapeDtypeStruct((b, ts, d), F32),
        grid=(b,),
        in_specs=[lam_spec] * 4 + [new, cache, cache, new, new, new, _resident((d, d)),
                                   _resident((1, 2 * dh))],
        out_specs=new,
        scratch_shapes=[pltpu.VMEM((ts, d), BF16)],
        compiler_params=_params("parallel"),
        name="da_attn_sample",
    )(*lams, q, k_rows, v_rows, kb, vb, x, wo, subln.reshape(1, 2 * dh))


def _rope_tables(pos, dk):
    angle = 1.0 / (ROPE_BASE ** jnp.linspace(0.0, 1.0, dk // 2, dtype=F32))
    angle = jnp.repeat(angle, 2)
    theta = pos[:, None] * angle[None, :]
    sign = jnp.where(jnp.arange(dk) % 2 == 0, -1.0, 1.0).astype(F32)
    return jnp.cos(theta), jnp.sin(theta) * sign[None, :]


def _ret_proj_kernel(x_ref, g_ref, w_ref, cos_ref, sin_ref, q_ref, k_ref, v_ref, gate_ref, *,
                     n_heads, dk, k_scale):
    nq = n_heads * dk
    nv = v_ref.shape[1]
    h = _rmsnorm(x_ref[...], g_ref[...], NORM_EPS).astype(BF16)
    cos = cos_ref[...]
    sin = sin_ref[...]
    even = lax.broadcasted_iota(jnp.int32, cos.shape, 1) % 2 == 0

    def rotary(a):
        rot = jnp.where(even, pltpu.roll(a, dk - 1, 1), pltpu.roll(a, 1, 1))
        return a * cos + rot * sin

    q = _dot(h, w_ref[:, 0:nq])
    k = _dot(h, w_ref[:, nq:2 * nq])
    for hd in range(n_heads):
        sl = slice(hd * dk, (hd + 1) * dk)
        q_ref[:, sl] = rotary(q[:, sl]).astype(BF16)
        k_ref[:, sl] = (rotary(k[:, sl]) * k_scale).astype(BF16)
    v_ref[...] = _dot(h, w_ref[:, 2 * nq:2 * nq + nv]).astype(BF16)
    gate_ref[...] = _dot(h, w_ref[:, 2 * nq + nv:2 * nq + 2 * nv])


def _ret_proj(x, g, w, cos, sin, n_heads, dk, dv):
    n, d = x.shape
    nq, nv = n_heads * dk, n_heads * dv
    tm = _row_tile(cos.shape[0])
    pos_blocks = cos.shape[0] // tm
    row = lambda width: pl.BlockSpec((tm, width), lambda i: (i, 0))
    tab = pl.BlockSpec((tm, dk), lambda i: (i % pos_blocks, 0))
    return pl.pallas_call(
        functools.partial(_ret_proj_kernel, n_heads=n_heads, dk=dk, k_scale=dk ** -0.5),
        out_shape=(jax.ShapeDtypeStruct((n, nq), BF16), jax.ShapeDtypeStruct((n, nq), BF16),
                   jax.ShapeDtypeStruct((n, nv), BF16), jax.ShapeDtypeStruct((n, nv), F32)),
        grid=(n // tm,),
        in_specs=[row(d), _resident((1, d)), _resident((d, 2 * nq + 2 * nv)), tab, tab],
        out_specs=(row(nq), row(nq), row(nv), row(nv)),
        compiler_params=_params("parallel"),
        name="ret_proj",
    )(x, g.reshape(1, d), w, cos, sin)


def _ret_core_kernel(lg_ref, q_ref, k_ref, v_ref, gate_ref, x_ref, s0_ref, wo_ref, o_ref,
                     sfin_ref, s_scr, y_scr, *, n_heads, dk, dv, blk):
    c = pl.program_id(1)

    @pl.when(c == 0)
    def _():
        s_scr[...] = s0_ref[0]

    row = lax.broadcasted_iota(jnp.int32, (blk, blk), 0)
    col = lax.broadcasted_iota(jnp.int32, (blk, blk), 1)
    rel = (row - col).astype(F32)
    idx = lax.broadcasted_iota(jnp.int32, (blk, 1), 0).astype(F32)

    for h in range(n_heads):
        lg = lg_ref[h]
        dmask = jnp.where(rel >= 0, jnp.exp(jnp.maximum(rel, 0.0) * lg), 0.0)
        qh = q_ref[0, :, h * dk:(h + 1) * dk]
        kh = k_ref[0, :, h * dk:(h + 1) * dk]
        vh = v_ref[0, :, h * dv:(h + 1) * dv]
        s = s_scr[h]
        qk = _dot_nt(qh, kh) * dmask
        inner = _dot(qk.astype(BF16), vh)
        cross = _dot(qh, s.astype(BF16)) * jnp.exp((idx + 1.0) * lg)
        o = inner + cross
        kd = (kh.astype(F32) * jnp.exp((blk - 1.0 - idx) * lg)).astype(BF16)
        s_decay = jnp.exp(jnp.full((1, dv), float(blk), F32) * lg)
        s_scr[h] = s * s_decay + _dot_tn(kd, vh)
        on = o * lax.rsqrt(jnp.mean(o * o, axis=-1, keepdims=True) + RET_GN_EPS)
        gate = gate_ref[0, :, h * dv:(h + 1) * dv]
        y_scr[:, h * dv:(h + 1) * dv] = (gate * jax.nn.sigmoid(gate) * on).astype(BF16)

    o_ref[0] = x_ref[0] + _dot(y_scr[...], wo_ref[...])

    @pl.when(c == pl.num_programs(1) - 1)
    def _():
        sfin_ref[0] = s_scr[...]


def _ret_core(lg, q, k, v, gate, x, s0, wo, n_heads, dk, dv):
    b, t, d = x.shape
    blk = RET_TILE if t % RET_TILE == 0 else t
    nq, nv = n_heads * dk, n_heads * dv
    tok = lambda width: pl.BlockSpec((1, blk, width), lambda bi, c: (bi, c, 0))
    state = pl.BlockSpec((1, n_heads, dk, dv), lambda bi, c: (bi, 0, 0, 0))
    return pl.pallas_call(
        functools.partial(_ret_core_kernel, n_heads=n_heads, dk=dk, dv=dv, blk=blk),
        out_shape=(jax.ShapeDtypeStruct((b, t, d), F32),
                   jax.ShapeDtypeStruct((b, n_heads, dk, dv), F32)),
        grid=(b, t // blk),
        in_specs=[pl.BlockSpec(memory_space=pltpu.SMEM), tok(nq), tok(nq), tok(nv), tok(nv),
                  tok(d), state, _resident((nv, d))],
        out_specs=(tok(d), state),
        scratch_shapes=[pltpu.VMEM((n_heads, dk, dv), F32), pltpu.VMEM((blk, nv), BF16)],
        compiler_params=_params("parallel", "arbitrary"),
        name="ret_core",
    )(lg, q, k, v, gate, x, s0, wo)


def _lambda_init(layer_idx):
    return 0.8 - 0.6 * math.exp(-0.3 * layer_idx)


def kernel(x_prompt, x_sample, cache_diff_k, cache_diff_v, state_ret, ffn1_norm, ffn1_w_gate, ffn1_w_up, ffn1_w_down, mix_norm, da_w_qkv, da_lambda_q1, da_lambda_k1, da_lambda_q2, da_lambda_k2, da_subln, da_w_o, ret_w_in, ret_w_o, ffn2_norm, ffn2_w_gate, ffn2_w_up, ffn2_w_down, final_norm):
    bp, tp, d = x_prompt.shape
    bs, ts, _ = x_sample.shape
    past = cache_diff_k.shape[2]
    da_heads = cache_diff_k.shape[3]
    dh = cache_diff_k.shape[5]
    ret_heads, dk, dv = state_ret.shape[2:]
    depth = ffn1_norm.shape[0]
    bf = lambda w: w.astype(BF16)

    xp = x_prompt.reshape(bp * tp, d)
    xs = x_sample.reshape(bs * ts, d)
    kp_list, vp_list, sp_list = [], [], []
    ks_list, vs_list, ss_list = [], [], []
    for i in range(depth):
        w1 = (ffn1_norm[i], bf(ffn1_w_gate[i]), bf(ffn1_w_up[i]), bf(ffn1_w_down[i]))
        xp = _ffn(xp, *w1)
        xs = _ffn(xs, *w1)
        if i % 2 == 0:
            a = i // 2
            lam_init = _lambda_init(i)
            w_qkv, w_o = bf(da_w_qkv[a]), bf(da_w_o[a])
            lams = [v[a].reshape(1, dh) for v in (da_lambda_q1, da_lambda_k1, da_lambda_q2, da_lambda_k2)]
            w_vt = bf(da_w_qkv[a][:, 2 * d:].T)
            qp, kp, vp, kpb, vpt = _da_proj(xp, mix_norm[i], w_qkv, w_vt, dh ** -0.5, True, da_heads)
            qs, ks, vs, ksb, vsb = _da_proj(xs, mix_norm[i], w_qkv, w_vt, dh ** -0.5, False, da_heads)
            seq = lambda z, b, t: z.reshape(b, t, d)
            xp = _da_attn_prompt(lams, seq(qp, bp, tp), seq(kpb, bp, tp), vpt,
                                 seq(xp, bp, tp), w_o, da_subln[a], da_heads,
                                 lam_init).reshape(bp * tp, d)
            xs = _da_attn_sample(lams, seq(qs, bs, ts), _key_cache_to_rows(cache_diff_k[a]),
                                 _value_cache_to_rows(cache_diff_v[a]), seq(ksb, bs, ts),
                                 seq(vsb, bs, ts), seq(xs, bs, ts), w_o, da_subln[a], da_heads,
                                 lam_init).reshape(bs * ts, d)
            kp_list.append(_key_rows_to_cache(kp, bp, tp, da_heads, dh))
            vp_list.append(_value_rows_to_cache(vp, bp, tp, da_heads, dh))
            ks_list.append(_key_rows_to_cache(ks, bs, ts, da_heads, dh))
            vs_list.append(_value_rows_to_cache(vs, bs, ts, da_heads, dh))
        else:
            r = i // 2
            w_in, w_o = bf(ret_w_in[r]), bf(ret_w_o[r])
            lg = jnp.log1p(-jnp.exp2(-5.0 - jnp.arange(ret_heads, dtype=F32)))
            nq, nv = ret_heads * dk, ret_heads * dv
            groups = ((xp, bp, tp, jnp.arange(tp, dtype=F32),
                       jnp.zeros((bp, ret_heads, dk, dv), F32), sp_list),
                      (xs, bs, ts, past + jnp.arange(ts, dtype=F32),
                       state_ret[r].astype(F32), ss_list))
            outs = []
            for x, b, t, pos, s0, s_list in groups:
                cos, sin = _rope_tables(pos, dk)
                if t % ROW_TILE != 0:
                    cos, sin = jnp.tile(cos, (b, 1)), jnp.tile(sin, (b, 1))
                q, k, v, gate = _ret_proj(x, mix_norm[i], w_in, cos, sin, ret_heads, dk, dv)
                y, s_fin = _ret_core(lg, q.reshape(b, t, nq), k.reshape(b, t, nq),
                                     v.reshape(b, t, nv), gate.reshape(b, t, nv),
                                     x.reshape(b, t, d), s0, w_o, ret_heads, dk, dv)
                outs.append(y.reshape(b * t, d))
                s_list.append(s_fin)
            xp, xs = outs
        fin = final_norm if i == depth - 1 else None
        w2 = (ffn2_norm[i], bf(ffn2_w_gate[i]), bf(ffn2_w_up[i]), bf(ffn2_w_down[i]))
        xp = _ffn(xp, *w2, final_g=fin)
        xs = _ffn(xs, *w2, final_g=fin)

    return (xp.reshape(bp, tp, d), xs.reshape(bs, ts, d),
            jnp.stack(kp_list), jnp.stack(vp_list), jnp.stack(sp_list),
            jnp.stack(ks_list), jnp.stack(vs_list), jnp.stack(ss_list))
```

```python
import functools
import math

import jax
import jax.numpy as jnp
from jax import lax
from jax.experimental import pallas as pl
from jax.experimental.pallas import tpu as pltpu

F32 = jnp.float32
BF16 = jnp.bfloat16

NORM_EPS = 1e-6
DA_SUBLN_EPS = 1e-5
RET_GN_EPS = 1e-6
ROPE_BASE = 10000.0
CHUNK = 64
NEG_INF = -1e30

LANES = 128
V7X_VMEM_BYTES = 64 * 1024 * 1024
VMEM_LIMIT_BYTES = (V7X_VMEM_BYTES * 3) // 4

ROW_TILE = 512
ATTN_TILE = 512
RET_TILE = 256
SCORE_LOOKAHEAD = 2
SUM_ROWS = 16


def _params(*semantics):
    return pltpu.CompilerParams(dimension_semantics=semantics,
                                vmem_limit_bytes=VMEM_LIMIT_BYTES)


def _resident(shape):
    return pl.BlockSpec(shape, lambda *_: (0,) * len(shape),
                        pipeline_mode=pl.Buffered(1))


def _rmsnorm(x, g, eps):
    return x * lax.rsqrt(jnp.mean(x * x, axis=-1, keepdims=True) + eps) * g


def _dot(a, b):
    return jnp.dot(a, b, preferred_element_type=F32)


def _dot_nt(a, b):
    return lax.dot_general(a, b, (((1,), (1,)), ((), ())), preferred_element_type=F32)


def _dot_tn(a, b):
    return lax.dot_general(a, b, (((0,), (0,)), ((), ())), preferred_element_type=F32)


def _row_tile(n):
    return ROW_TILE if n % ROW_TILE == 0 else n


def _ffn_kernel(x_ref, g_ref, wg_ref, wu_ref, wd_ref, *rest, f_tile, final):
    if final:
        fg_ref, o_ref = rest
    else:
        (o_ref,) = rest
    x = x_ref[...]
    h = _rmsnorm(x, g_ref[...], NORM_EPS).astype(BF16)
    acc = None
    for c in range(wg_ref.shape[1] // f_tile):
        sl = slice(c * f_tile, (c + 1) * f_tile)
        gate = _dot(h, wg_ref[:, sl])
        up = _dot(h, wu_ref[:, sl])
        a = (gate * jax.nn.sigmoid(gate) * up).astype(BF16)
        d = _dot(a, wd_ref[sl, :])
        acc = d if acc is None else acc + d
    y = x + 0.5 * acc
    if final:
        y = _rmsnorm(y, fg_ref[...], NORM_EPS)
    o_ref[...] = y


def _ffn(x, g, wg, wu, wd, final_g=None):
    n, d = x.shape
    f = wg.shape[1]
    tm = _row_tile(n)
    f_tile = 256 if f % 256 == 0 else f
    final = final_g is not None
    row = pl.BlockSpec((tm, d), lambda i: (i, 0))
    in_specs = [row, _resident((1, d)), _resident((d, f)), _resident((d, f)), _resident((f, d))]
    args = [x, g.reshape(1, d), wg, wu, wd]
    if final:
        in_specs.append(_resident((1, d)))
        args.append(final_g.reshape(1, d))
    return pl.pallas_call(
        functools.partial(_ffn_kernel, f_tile=f_tile, final=final),
        out_shape=jax.ShapeDtypeStruct((n, d), F32),
        grid=(n // tm,),
        in_specs=in_specs,
        out_specs=row,
        compiler_params=_params("parallel"),
        name="ffn_final" if final else "ffn",
    )(*args)


def _da_proj_kernel(x_ref, g_ref, w_ref, wvt_ref, q_ref, k_ref, v_ref, kb_ref, vb_ref, *,
                    q_scale, transposed_v, n_heads):
    tm, d = x_ref.shape
    lane_tiles = d // LANES
    h = _rmsnorm(x_ref[...], g_ref[...], NORM_EPS).astype(BF16)
    q_ref[...] = (_dot(h, w_ref[:, 0:d]) * q_scale).astype(BF16)
    k = _dot(h, w_ref[:, d:2 * d])
    kb_ref[...] = k.astype(BF16)
    v = _dot(h, w_ref[:, 2 * d:3 * d])
    for j in range(lane_tiles):
        cols = slice(j * LANES, (j + 1) * LANES)
        k_ref[pl.ds(j, tm, stride=lane_tiles), :] = k[:, cols]
        head, half = divmod(j, lane_tiles // n_heads)
        v_ref[pl.ds(half * n_heads + head, tm, stride=lane_tiles), :] = v[:, cols]
    if transposed_v:
        vb_ref[...] = _dot_nt(wvt_ref[...], h).astype(BF16)
    else:
        vb_ref[...] = v.astype(BF16)


def _da_proj(x, g, w, wvt, q_scale, transposed_v, n_heads):
    n, d = x.shape
    tm = _row_tile(n)
    lane_tiles = d // LANES
    row = pl.BlockSpec((tm, d), lambda i: (i, 0))
    rows = pl.BlockSpec((tm * lane_tiles, LANES), lambda i: (i, 0))
    if transposed_v:
        vb_shape, vb_spec = (d, n), pl.BlockSpec((d, tm), lambda i: (0, i))
    else:
        vb_shape, vb_spec = (n, d), row
    return pl.pallas_call(
        functools.partial(_da_proj_kernel, q_scale=q_scale, transposed_v=transposed_v,
                          n_heads=n_heads),
        out_shape=(jax.ShapeDtypeStruct((n, d), BF16),
                   jax.ShapeDtypeStruct((n * lane_tiles, LANES), F32),
                   jax.ShapeDtypeStruct((n * lane_tiles, LANES), F32),
                   jax.ShapeDtypeStruct((n, d), BF16),
                   jax.ShapeDtypeStruct(vb_shape, BF16)),
        grid=(n // tm,),
        in_specs=[row, _resident((1, d)), _resident((d, 3 * d)), _resident((d, d))],
        out_specs=(row, rows, rows, row, vb_spec),
        compiler_params=_params("parallel"),
        name="da_proj_t" if transposed_v else "da_proj",
    )(x, g.reshape(1, d), w, wvt)


def _key_rows_to_cache(k_rows, b, t, n_heads, dh):
    return k_rows.reshape(b, t, n_heads, 2, dh)


def _value_rows_to_cache(v_rows, b, t, n_heads, dh):
    v = v_rows.reshape(b, t, 2, n_heads, dh)
    return jnp.swapaxes(v, 2, 3).reshape(b, t, n_heads, 2 * dh)


def _key_cache_to_rows(k_cache):
    b, t, n_heads, _, dh = k_cache.shape
    return k_cache.reshape(b, t * n_heads * 2, dh)


def _value_cache_to_rows(v_cache):
    b, t, n_heads, dv = v_cache.shape
    v = v_cache.reshape(b, t, n_heads, 2, dv // 2)
    return jnp.swapaxes(v, 2, 3).reshape(b, t * 2 * n_heads, dv // 2)


def _da_lambda(lq1_ref, lk1_ref, lq2_ref, lk2_ref, lam_init):
    s1 = jnp.sum(lq1_ref[...] * lk1_ref[...], axis=1, keepdims=True)
    s2 = jnp.sum(lq2_ref[...] * lk2_ref[...], axis=1, keepdims=True)
    return jnp.exp(s1) - jnp.exp(s2) + lam_init


def _da_head_out(o, sub_ref, lam_init):
    return (_rmsnorm(o, sub_ref[...], DA_SUBLN_EPS) * (1.0 - lam_init)).astype(BF16)


def _da_attn_kernel(lq1_ref, lk1_ref, lq2_ref, lk2_ref, q_ref, k_ref, vt_ref, x_ref, wo_ref,
                    sub_ref, o_ref, acc_ref, cat_ref, *, n_heads, dh, tq, lam_init):
    i = pl.program_id(1)
    dv = 2 * dh
    lam = _da_lambda(lq1_ref, lk1_ref, lq2_ref, lk2_ref, lam_init)
    key_chunk = lax.broadcasted_iota(jnp.int32, (tq, tq), 0) // CHUNK
    qry_chunk = lax.broadcasted_iota(jnp.int32, (tq, tq), 1) // CHUNK
    diag_mask = key_chunk <= qry_chunk

    acc_ref[...] = jnp.zeros_like(acc_ref)

    def absorb(j, stats, masked):
        ks = pl.multiple_of(j * tq, tq)
        new = []
        n_chains = 2 * n_heads

        def scores(chain):
            cols = slice(chain * dh, (chain + 1) * dh)
            return _dot_nt(k_ref[0, pl.ds(ks, tq), cols], q_ref[0, :, cols])

        pending = [scores(c) for c in range(SCORE_LOOKAHEAD)]
        ones = jnp.ones((SUM_ROWS, tq), BF16)
        for chain in range(n_chains):
            h = chain // 2
            s = pending.pop(0)
            if chain + SCORE_LOOKAHEAD < n_chains:
                pending.append(scores(chain + SCORE_LOOKAHEAD))
            if masked:
                s = jnp.where(diag_mask, s, NEG_INF)
            m_new = jnp.maximum(stats[chain], jnp.max(s, axis=0, keepdims=True))
            alpha = jnp.exp2(stats[chain] - m_new)
            p = jnp.exp2(s - m_new).astype(BF16)
            vt = jnp.concatenate([vt_ref[h * dv:(h + 1) * dv, pl.ds(ks, tq)], ones], axis=0)
            acc_ref[chain] = alpha * acc_ref[chain] + _dot(vt, p)
            new.append(m_new)
        return tuple(new)

    m0 = jnp.full((1, tq), NEG_INF, F32)
    stats = lax.fori_loop(0, i, functools.partial(absorb, masked=False), (m0,) * (2 * n_heads))
    absorb(i, stats, masked=True)

    for h in range(n_heads):
        l1, l2 = acc_ref[2 * h, dv:dv + 1, :], acc_ref[2 * h + 1, dv:dv + 1, :]
        o = acc_ref[2 * h, :dv, :] / l1 - lam * (acc_ref[2 * h + 1, :dv, :] / l2)
        on = o * lax.rsqrt(jnp.mean(o * o, axis=0, keepdims=True) + DA_SUBLN_EPS) * sub_ref[...]
        cat_ref[h * dv:(h + 1) * dv, :] = (on * (1.0 - lam_init)).astype(BF16)

    o_ref[0] = x_ref[0] + _dot_tn(cat_ref[...], wo_ref[...])


def _da_attn_prompt(lams, q, kb, vt, x, wo, subln, n_heads, lam_init):
    b, t, d = x.shape
    dh = d // (2 * n_heads)
    tq = ATTN_TILE if t % ATTN_TILE == 0 else t
    assert tq % CHUNK == 0
    blk = pl.BlockSpec((1, tq, d), lambda bi, i: (bi, i, 0))
    seq = pl.BlockSpec((1, t, d), lambda bi, i: (bi, 0, 0), pipeline_mode=pl.Buffered(1))
    seq_t = pl.BlockSpec((d, t), lambda bi, i: (0, bi), pipeline_mode=pl.Buffered(1))
    lam_spec = _resident((1, dh))
    return pl.pallas_call(
        functools.partial(_da_attn_kernel, n_heads=n_heads, dh=dh, tq=tq, lam_init=lam_init),
        out_shape=jax.ShapeDtypeStruct((b, t, d), F32),
        grid=(b, t // tq),
        in_specs=[lam_spec] * 4 + [blk, seq, seq_t, blk, _resident((d, d)),
                                   _resident((2 * dh, 1))],
        out_specs=blk,
        scratch_shapes=[pltpu.VMEM((2 * n_heads, 2 * dh + SUM_ROWS, tq), F32),
                        pltpu.VMEM((d, tq), BF16)],
        compiler_params=_params("parallel", "arbitrary"),
        name="da_attn_prompt",
    )(*lams, q, kb, vt, x, wo, subln.reshape(2 * dh, 1))


def _da_attn_sample_kernel(lq1_ref, lk1_ref, lq2_ref, lk2_ref, q_ref, kc_ref, vc_ref, kn_ref,
                           vn_ref, x_ref, wo_ref, sub_ref, o_ref, cat_ref, *, n_heads, dh,
                           lam_init):
    lam = _da_lambda(lq1_ref, lk1_ref, lq2_ref, lk2_ref, lam_init)
    rows_per_token = 2 * n_heads
    past = kc_ref.shape[1] // rows_per_token

    def cached(ref, row):
        return ref.at[0][pl.ds(row, past, stride=rows_per_token), :].astype(BF16)

    for h in range(n_heads):
        c0 = h * 2 * dh
        probs = []
        for c in range(2):
            qh = q_ref[0, :, c0 + c * dh:c0 + (c + 1) * dh]
            sc = _dot_nt(qh, cached(kc_ref, 2 * h + c))
            sn = _dot_nt(qh, kn_ref[0, :, c0 + c * dh:c0 + (c + 1) * dh])
            m = jnp.maximum(jnp.max(sc, axis=1, keepdims=True), jnp.max(sn, axis=1, keepdims=True))
            pc = jnp.exp2(sc - m)
            pn = jnp.exp2(sn - m)
            l = jnp.sum(pc, axis=1, keepdims=True) + jnp.sum(pn, axis=1, keepdims=True)
            probs.append((pc / l, pn / l))
        ac = (probs[0][0] - lam * probs[1][0]).astype(BF16)
        an = (probs[0][1] - lam * probs[1][1]).astype(BF16)
        vc = jnp.concatenate([cached(vc_ref, h), cached(vc_ref, n_heads + h)], axis=1)
        o = _dot(ac, vc) + _dot(an, vn_ref[0, :, c0:c0 + 2 * dh])
        cat_ref[:, c0:c0 + 2 * dh] = _da_head_out(o, sub_ref, lam_init)
    o_ref[0] = x_ref[0] + _dot(cat_ref[...], wo_ref[...])


def _da_attn_sample(lams, q, k_rows, v_rows, kb, vb, x, wo, subln, n_heads, lam_init):
    b, ts, d = x.shape
    dh = d // (2 * n_heads)
    past = k_rows.shape[1] // (2 * n_heads)
    assert past % CHUNK == 0 and ts <= CHUNK
    new = pl.BlockSpec((1, ts, d), lambda bi: (bi, 0, 0))
    cache = pl.BlockSpec((1, past * 2 * n_heads, dh), lambda bi: (bi, 0, 0))
    lam_spec = _resident((1, dh))
    return pl.pallas_call(
        functools.partial(_da_attn_sample_kernel, n_heads=n_heads, dh=dh, lam_init=lam_init),
        out_shape=jax.ShapeDtypeStruct((b, ts, d), F32),
        grid=(b,),
        in_specs=[lam_spec] * 4 + [new, cache, cache, new, new, new, _resident((d, d)),
                                   _resident((1, 2 * dh))],
        out_specs=new,
        scratch_shapes=[pltpu.VMEM((ts, d), BF16)],
        compiler_params=_params("parallel"),
        name="da_attn_sample",
    )(*lams, q, k_rows, v_rows, kb, vb, x, wo, subln.reshape(1, 2 * dh))


def _rope_tables(pos, dk):
    angle = 1.0 / (ROPE_BASE ** jnp.linspace(0.0, 1.0, dk // 2, dtype=F32))
    angle = jnp.repeat(angle, 2)
    theta = pos[:, None] * angle[None, :]
    sign = jnp.where(jnp.arange(dk) % 2 == 0, -1.0, 1.0).astype(F32)
    return jnp.cos(theta), jnp.sin(theta) * sign[None, :]


def _ret_proj_kernel(x_ref, g_ref, w_ref, cos_ref, sin_ref, q_ref, k_ref, v_ref, gate_ref, *,
                     n_heads, dk, k_scale):
    nq = n_heads * dk
    nv = v_ref.shape[1]
    h = _rmsnorm(x_ref[...], g_ref[...], NORM_EPS).astype(BF16)
    cos = cos_ref[...]
    sin = sin_ref[...]
    even = lax.broadcasted_iota(jnp.int32, cos.shape, 1) % 2 == 0

    def rotary(a):
        rot = jnp.where(even, pltpu.roll(a, dk - 1, 1), pltpu.roll(a, 1, 1))
        return a * cos + rot * sin

    q = _dot(h, w_ref[:, 0:nq])
    k = _dot(h, w_ref[:, nq:2 * nq])
    for hd in range(n_heads):
        sl = slice(hd * dk, (hd + 1) * dk)
        q_ref[:, sl] = rotary(q[:, sl]).astype(BF16)
        k_ref[:, sl] = (rotary(k[:, sl]) * k_scale).astype(BF16)
    v_ref[...] = _dot(h, w_ref[:, 2 * nq:2 * nq + nv]).astype(BF16)
    gate_ref[...] = _dot(h, w_ref[:, 2 * nq + nv:2 * nq + 2 * nv])


def _ret_proj(x, g, w, cos, sin, n_heads, dk, dv):
    n, d = x.shape
    nq, nv = n_heads * dk, n_heads * dv
    tm = _row_tile(cos.shape[0])
    pos_blocks = cos.shape[0] // tm
    row = lambda width: pl.BlockSpec((tm, width), lambda i: (i, 0))
    tab = pl.BlockSpec((tm, dk), lambda i: (i % pos_blocks, 0))
    return pl.pallas_call(
        functools.partial(_ret_proj_kernel, n_heads=n_heads, dk=dk, k_scale=dk ** -0.5),
        out_shape=(jax.ShapeDtypeStruct((n, nq), BF16), jax.ShapeDtypeStruct((n, nq), BF16),
                   jax.ShapeDtypeStruct((n, nv), BF16), jax.ShapeDtypeStruct((n, nv), F32)),
        grid=(n // tm,),
        in_specs=[row(d), _resident((1, d)), _resident((d, 2 * nq + 2 * nv)), tab, tab],
        out_specs=(row(nq), row(nq), row(nv), row(nv)),
        compiler_params=_params("parallel"),
        name="ret_proj",
    )(x, g.reshape(1, d), w, cos, sin)


def _ret_core_kernel(lg_ref, q_ref, k_ref, v_ref, gate_ref, x_ref, s0_ref, wo_ref, o_ref,
                     sfin_ref, s_scr, y_scr, *, n_heads, dk, dv, blk):
    c = pl.program_id(1)

    @pl.when(c == 0)
    def _():
        s_scr[...] = s0_ref[0]

    row = lax.broadcasted_iota(jnp.int32, (blk, blk), 0)
    col = lax.broadcasted_iota(jnp.int32, (blk, blk), 1)
    rel = (row - col).astype(F32)
    idx = lax.broadcasted_iota(jnp.int32, (blk, 1), 0).astype(F32)

    for h in range(n_heads):
        lg = lg_ref[h]
        dmask = jnp.where(rel >= 0, jnp.exp(jnp.maximum(rel, 0.0) * lg), 0.0)
        qh = q_ref[0, :, h * dk:(h + 1) * dk]
        kh = k_ref[0, :, h * dk:(h + 1) * dk]
        vh = v_ref[0, :, h * dv:(h + 1) * dv]
        s = s_scr[h]
        qk = _dot_nt(qh, kh) * dmask
        inner = _dot(qk.astype(BF16), vh)
        cross = _dot(qh, s.astype(BF16)) * jnp.exp((idx + 1.0) * lg)
        o = inner + cross
        kd = (kh.astype(F32) * jnp.exp((blk - 1.0 - idx) * lg)).astype(BF16)
        s_decay = jnp.exp(jnp.full((1, dv), float(blk), F32) * lg)
        s_scr[h] = s * s_decay + _dot_tn(kd, vh)
        on = o * lax.rsqrt(jnp.mean(o * o, axis=-1, keepdims=True) + RET_GN_EPS)
        gate = gate_ref[0, :, h * dv:(h + 1) * dv]
        y_scr[:, h * dv:(h + 1) * dv] = (gate * jax.nn.sigmoid(gate) * on).astype(BF16)

    o_ref[0] = x_ref[0] + _dot(y_scr[...], wo_ref[...])

    @pl.when(c == pl.num_programs(1) - 1)
    def _():
        sfin_ref[0] = s_scr[...]


def _ret_core(lg, q, k, v, gate, x, s0, wo, n_heads, dk, dv):
    b, t, d = x.shape
    blk = RET_TILE if t % RET_TILE == 0 else t
    nq, nv = n_heads * dk, n_heads * dv
    tok = lambda width: pl.BlockSpec((1, blk, width), lambda bi, c: (bi, c, 0))
    state = pl.BlockSpec((1, n_heads, dk, dv), lambda bi, c: (bi, 0, 0, 0))
    return pl.pallas_call(
        functools.partial(_ret_core_kernel, n_heads=n_heads, dk=dk, dv=dv, blk=blk),
        out_shape=(jax.ShapeDtypeStruct((b, t, d), F32),
                   jax.ShapeDtypeStruct((b, n_heads, dk, dv), F32)),
        grid=(b, t // blk),
        in_specs=[pl.BlockSpec(memory_space=pltpu.SMEM), tok(nq), tok(nq), tok(nv), tok(nv),
                  tok(d), state, _resident((nv, d))],
        out_specs=(tok(d), state),
        scratch_shapes=[pltpu.VMEM((n_heads, dk, dv), F32), pltpu.VMEM((blk, nv), BF16)],
        compiler_params=_params("parallel", "arbitrary"),
        name="ret_core",
    )(lg, q, k, v, gate, x, s0, wo)


def _lambda_init(layer_idx):
    return 0.8 - 0.6 * math.exp(-0.3 * layer_idx)


def kernel(x_prompt, x_sample, cache_diff_k, cache_diff_v, state_ret, ffn1_norm, ffn1_w_gate, ffn1_w_up, ffn1_w_down, mix_norm, da_w_qkv, da_lambda_q1, da_lambda_k1, da_lambda_q2, da_lambda_k2, da_subln, da_w_o, ret_w_in, ret_w_o, ffn2_norm, ffn2_w_gate, ffn2_w_up, ffn2_w_down, final_norm):
    bp, tp, d = x_prompt.shape
    bs, ts, _ = x_sample.shape
    past = cache_diff_k.shape[2]
    da_heads = cache_diff_k.shape[3]
    dh = cache_diff_k.shape[5]
    ret_heads, dk, dv = state_ret.shape[2:]
    depth = ffn1_norm.shape[0]
    bf = lambda w: w.astype(BF16)

    xp = x_prompt.reshape(bp * tp, d)
    xs = x_sample.reshape(bs * ts, d)
    kp_list, vp_list, sp_list = [], [], []
    ks_list, vs_list, ss_list = [], [], []
    for i in range(depth):
        w1 = (ffn1_norm[i], bf(ffn1_w_gate[i]), bf(ffn1_w_up[i]), bf(ffn1_w_down[i]))
        xp = _ffn(xp, *w1)
        xs = _ffn(xs, *w1)
        if i % 2 == 0:
            a = i // 2
            lam_init = _lambda_init(i)
            w_qkv, w_o = bf(da_w_qkv[a]), bf(da_w_o[a])
            lams = [v[a].reshape(1, dh) for v in (da_lambda_q1, da_lambda_k1, da_lambda_q2, da_lambda_k2)]
            w_vt = bf(da_w_qkv[a][:, 2 * d:].T)
            q_scale = dh ** -0.5 * math.log2(math.e)
            qp, kp, vp, kpb, vpt = _da_proj(xp, mix_norm[i], w_qkv, w_vt, q_scale, True, da_heads)
            qs, ks, vs, ksb, vsb = _da_proj(xs, mix_norm[i], w_qkv, w_vt, q_scale, False, da_heads)
            seq = lambda z, b, t: z.reshape(b, t, d)
            xp = _da_attn_prompt(lams, seq(qp, bp, tp), seq(kpb, bp, tp), vpt,
                                 seq(xp, bp, tp), w_o, da_subln[a], da_heads,
                                 lam_init).reshape(bp * tp, d)
            xs = _da_attn_sample(lams, seq(qs, bs, ts), _key_cache_to_rows(cache_diff_k[a]),
                                 _value_cache_to_rows(cache_diff_v[a]), seq(ksb, bs, ts),
                                 seq(vsb, bs, ts), seq(xs, bs, ts), w_o, da_subln[a], da_heads,
                                 lam_init).reshape(bs * ts, d)
            kp_list.append(_key_rows_to_cache(kp, bp, tp, da_heads, dh))
            vp_list.append(_value_rows_to_cache(vp, bp, tp, da_heads, dh))
            ks_list.append(_key_rows_to_cache(ks, bs, ts, da_heads, dh))
            vs_list.append(_value_rows_to_cache(vs, bs, ts, da_heads, dh))
        else:
            r = i // 2
            w_in, w_o = bf(ret_w_in[r]), bf(ret_w_o[r])
            lg = jnp.log1p(-jnp.exp2(-5.0 - jnp.arange(ret_heads, dtype=F32)))
            nq, nv = ret_heads * dk, ret_heads * dv
            groups = ((xp, bp, tp, jnp.arange(tp, dtype=F32),
                       jnp.zeros((bp, ret_heads, dk, dv), F32), sp_list),
                      (xs, bs, ts, past + jnp.arange(ts, dtype=F32),
                       state_ret[r].astype(F32), ss_list))
            outs = []
            for x, b, t, pos, s0, s_list in groups:
                cos, sin = _rope_tables(pos, dk)
                if t % ROW_TILE != 0:
                    cos, sin = jnp.tile(cos, (b, 1)), jnp.tile(sin, (b, 1))
                q, k, v, gate = _ret_proj(x, mix_norm[i], w_in, cos, sin, ret_heads, dk, dv)
                y, s_fin = _ret_core(lg, q.reshape(b, t, nq), k.reshape(b, t, nq),
                                     v.reshape(b, t, nv), gate.reshape(b, t, nv),
                                     x.reshape(b, t, d), s0, w_o, ret_heads, dk, dv)
                outs.append(y.reshape(b * t, d))
                s_list.append(s_fin)
            xp, xs = outs
        fin = final_norm if i == depth - 1 else None
        w2 = (ffn2_norm[i], bf(ffn2_w_gate[i]), bf(ffn2_w_up[i]), bf(ffn2_w_down[i]))
        xp = _ffn(xp, *w2, final_g=fin)
        xs = _ffn(xs, *w2, final_g=fin)

    return (xp.reshape(bp, tp, d), xs.reshape(bs, ts, d),
            jnp.stack(kp_list), jnp.stack(vp_list), jnp.stack(sp_list),
            jnp.stack(ks_list), jnp.stack(vs_list), jnp.stack(ss_list))
```

```python
import functools
import math

import jax
import jax.numpy as jnp
from jax import lax
from jax.experimental import pallas as pl
from jax.experimental.pallas import tpu as pltpu

F32 = jnp.float32
BF16 = jnp.bfloat16

NORM_EPS = 1e-6
DA_SUBLN_EPS = 1e-5
RET_GN_EPS = 1e-6
ROPE_BASE = 10000.0
CHUNK = 64
NEG_INF = -1e30

LANES = 128
V7X_VMEM_BYTES = 64 * 1024 * 1024
VMEM_LIMIT_BYTES = (V7X_VMEM_BYTES * 3) // 4

ROW_TILE = 512
ATTN_TILE = 512
RET_TILE = 256
SCORE_LOOKAHEAD = 2
WEIGHT_STAGE_CHUNKS = 8
SUM_ROWS = 16


def _params(*semantics):
    return pltpu.CompilerParams(dimension_semantics=semantics,
                                vmem_limit_bytes=VMEM_LIMIT_BYTES)


def _resident(shape):
    return pl.BlockSpec(shape, lambda *_: (0,) * len(shape),
                        pipeline_mode=pl.Buffered(1))


def _rmsnorm(x, g, eps):
    return x * lax.rsqrt(jnp.mean(x * x, axis=-1, keepdims=True) + eps) * g


def _dot(a, b):
    return jnp.dot(a, b, preferred_element_type=F32)


def _dot_nt(a, b):
    return lax.dot_general(a, b, (((1,), (1,)), ((), ())), preferred_element_type=F32)


def _dot_tn(a, b):
    return lax.dot_general(a, b, (((0,), (0,)), ((), ())), preferred_element_type=F32)


def _row_tile(n):
    return ROW_TILE if n % ROW_TILE == 0 else n


def _stage_and_cast(src_hbm, dst_vmem, stage, sem):
    rows = stage.shape[1]
    n_chunks = src_hbm.shape[0] // rows

    def chunk_copy(c, slot):
        return pltpu.make_async_copy(src_hbm.at[pl.ds(c * rows, rows)], stage.at[slot], sem.at[slot])

    chunk_copy(0, 0).start()

    def body(c, carry):
        slot = lax.rem(c, 2)

        @pl.when(c + 1 < n_chunks)
        def _():
            chunk_copy(c + 1, 1 - slot).start()

        chunk_copy(c, slot).wait()
        dst_vmem[pl.ds(pl.multiple_of(c * rows, rows), rows), :] = stage[slot].astype(BF16)
        return carry

    lax.fori_loop(0, n_chunks, body, 0)


def _stage_shape(weight_shape):
    rows, cols = weight_shape
    assert rows % WEIGHT_STAGE_CHUNKS == 0
    return (2, rows // WEIGHT_STAGE_CHUNKS, cols)


def _ffn_body(x_ref, g_ref, wg_ref, wu_ref, wd_ref, fg_ref, o_ref, f_tile):
    x = x_ref[...]
    h = _rmsnorm(x, g_ref[...], NORM_EPS).astype(BF16)
    acc = None
    for c in range(wg_ref.shape[1] // f_tile):
        sl = slice(c * f_tile, (c + 1) * f_tile)
        gate = _dot(h, wg_ref[:, sl])
        up = _dot(h, wu_ref[:, sl])
        a = (gate * jax.nn.sigmoid(gate) * up).astype(BF16)
        d = _dot(a, wd_ref[sl, :])
        acc = d if acc is None else acc + d
    y = x + 0.5 * acc
    if fg_ref is not None:
        y = _rmsnorm(y, fg_ref[...], NORM_EPS)
    o_ref[...] = y


def _ffn_kernel(x_ref, g_ref, wg_ref, wu_ref, wd_ref, *rest, f_tile, final):
    fg_ref, o_ref = rest if final else (None,) + rest
    _ffn_body(x_ref, g_ref, wg_ref, wu_ref, wd_ref, fg_ref, o_ref, f_tile)


def _ffn_cast_kernel(x_ref, g_ref, wg_hbm, wu_hbm, wd_hbm, *rest, f_tile, final, layer):
    fg_ref = rest[0] if final else None
    o_ref, wg_out, wu_out, wd_out, wg_v, wu_v, wd_v, stage_in, stage_down, sem_in, sem_out = (
        rest[1:] if final else rest)
    i = pl.program_id(0)
    copies_out = [pltpu.make_async_copy(v, out, sem_out.at[k])
                  for k, (v, out) in enumerate(((wg_v, wg_out), (wu_v, wu_out), (wd_v, wd_out)))]

    @pl.when(i == 0)
    def _():
        for src, dst, stage, copy_out in ((wg_hbm, wg_v, stage_in, copies_out[0]),
                                          (wu_hbm, wu_v, stage_in, copies_out[1]),
                                          (wd_hbm, wd_v, stage_down, copies_out[2])):
            _stage_and_cast(src.at[layer], dst, stage, sem_in)
            copy_out.start()

    _ffn_body(x_ref, g_ref, wg_v, wu_v, wd_v, fg_ref, o_ref, f_tile)

    @pl.when(i == pl.num_programs(0) - 1)
    def _():
        for copy_out in copies_out:
            copy_out.wait()


def _ffn(x, g, wg, wu, wd, final_g=None, layer=None):
    n, d = x.shape
    f = wg.shape[-1]
    tm = _row_tile(n)
    f_tile = 256 if f % 256 == 0 else f
    final = final_g is not None
    cast = layer is not None
    row = pl.BlockSpec((tm, d), lambda i: (i, 0))
    hbm = pl.BlockSpec(memory_space=pl.ANY)
    w_specs = [hbm] * 3 if cast else [_resident((d, f)), _resident((d, f)), _resident((f, d))]
    in_specs = [row, _resident((1, d))] + w_specs
    args = [x, g.reshape(1, d), wg, wu, wd]
    if final:
        in_specs.append(_resident((1, d)))
        args.append(final_g.reshape(1, d))
    y_shape = jax.ShapeDtypeStruct((n, d), F32)
    name = "ffn_final" if final else "ffn"
    if not cast:
        return pl.pallas_call(
            functools.partial(_ffn_kernel, f_tile=f_tile, final=final),
            out_shape=y_shape, grid=(n // tm,), in_specs=in_specs, out_specs=row,
            compiler_params=_params("parallel"), name=name,
        )(*args)
    up_shape, down_shape = wg.shape[1:], wd.shape[1:]
    bf16_like = lambda shape: jax.ShapeDtypeStruct(shape, BF16)
    return pl.pallas_call(
        functools.partial(_ffn_cast_kernel, f_tile=f_tile, final=final, layer=layer),
        out_shape=(y_shape, bf16_like(up_shape), bf16_like(up_shape), bf16_like(down_shape)),
        grid=(n // tm,),
        in_specs=in_specs,
        out_specs=(row, hbm, hbm, hbm),
        scratch_shapes=[pltpu.VMEM(up_shape, BF16), pltpu.VMEM(up_shape, BF16),
                        pltpu.VMEM(down_shape, BF16), pltpu.VMEM(_stage_shape(up_shape), F32),
                        pltpu.VMEM(_stage_shape(down_shape), F32),
                        pltpu.SemaphoreType.DMA((2,)), pltpu.SemaphoreType.DMA((3,))],
        compiler_params=_params("arbitrary"),
        name=name + "_cast",
    )(*args)


def _da_proj_kernel(x_ref, g_ref, w_ref, wvt_ref, q_ref, k_ref, v_ref, kb_ref, vb_ref, *,
                    q_scale, transposed_v, n_heads):
    tm, d = x_ref.shape
    lane_tiles = d // LANES
    h = _rmsnorm(x_ref[...], g_ref[...], NORM_EPS).astype(BF16)
    q_ref[...] = (_dot(h, w_ref[:, 0:d]) * q_scale).astype(BF16)
    k = _dot(h, w_ref[:, d:2 * d])
    kb_ref[...] = k.astype(BF16)
    v = _dot(h, w_ref[:, 2 * d:3 * d])
    for j in range(lane_tiles):
        cols = slice(j * LANES, (j + 1) * LANES)
        k_ref[pl.ds(j, tm, stride=lane_tiles), :] = k[:, cols]
        head, half = divmod(j, lane_tiles // n_heads)
        v_ref[pl.ds(half * n_heads + head, tm, stride=lane_tiles), :] = v[:, cols]
    if transposed_v:
        vb_ref[...] = _dot_nt(wvt_ref[...], h).astype(BF16)
    else:
        vb_ref[...] = v.astype(BF16)


def _da_proj(x, g, w, wvt, q_scale, transposed_v, n_heads):
    n, d = x.shape
    tm = _row_tile(n)
    lane_tiles = d // LANES
    row = pl.BlockSpec((tm, d), lambda i: (i, 0))
    rows = pl.BlockSpec((tm * lane_tiles, LANES), lambda i: (i, 0))
    if transposed_v:
        vb_shape, vb_spec = (d, n), pl.BlockSpec((d, tm), lambda i: (0, i))
    else:
        vb_shape, vb_spec = (n, d), row
    return pl.pallas_call(
        functools.partial(_da_proj_kernel, q_scale=q_scale, transposed_v=transposed_v,
                          n_heads=n_heads),
        out_shape=(jax.ShapeDtypeStruct((n, d), BF16),
                   jax.ShapeDtypeStruct((n * lane_tiles, LANES), F32),
                   jax.ShapeDtypeStruct((n * lane_tiles, LANES), F32),
                   jax.ShapeDtypeStruct((n, d), BF16),
                   jax.ShapeDtypeStruct(vb_shape, BF16)),
        grid=(n // tm,),
        in_specs=[row, _resident((1, d)), _resident((d, 3 * d)), _resident((d, d))],
        out_specs=(row, rows, rows, row, vb_spec),
        compiler_params=_params("parallel"),
        name="da_proj_t" if transposed_v else "da_proj",
    )(x, g.reshape(1, d), w, wvt)


def _key_rows_to_cache(k_rows, b, t, n_heads, dh):
    return k_rows.reshape(b, t, n_heads, 2, dh)


def _value_rows_to_cache(v_rows, b, t, n_heads, dh):
    v = v_rows.reshape(b, t, 2, n_heads, dh)
    return jnp.swapaxes(v, 2, 3).reshape(b, t, n_heads, 2 * dh)


def _key_cache_to_rows(k_cache):
    b, t, n_heads, _, dh = k_cache.shape
    return k_cache.reshape(b, t * n_heads * 2, dh)


def _value_cache_to_rows(v_cache):
    b, t, n_heads, dv = v_cache.shape
    v = v_cache.reshape(b, t, n_heads, 2, dv // 2)
    return jnp.swapaxes(v, 2, 3).reshape(b, t * 2 * n_heads, dv // 2)


def _da_lambda(lq1_ref, lk1_ref, lq2_ref, lk2_ref, lam_init):
    s1 = jnp.sum(lq1_ref[...] * lk1_ref[...], axis=1, keepdims=True)
    s2 = jnp.sum(lq2_ref[...] * lk2_ref[...], axis=1, keepdims=True)
    return jnp.exp(s1) - jnp.exp(s2) + lam_init


def _da_head_out(o, sub_ref, lam_init):
    return (_rmsnorm(o, sub_ref[...], DA_SUBLN_EPS) * (1.0 - lam_init)).astype(BF16)


def _da_attn_kernel(lq1_ref, lk1_ref, lq2_ref, lk2_ref, q_ref, k_ref, vt_ref, x_ref, wo_ref,
                    sub_ref, o_ref, acc_ref, cat_ref, *, n_heads, dh, tq, lam_init):
    i = pl.program_id(1)
    dv = 2 * dh
    lam = _da_lambda(lq1_ref, lk1_ref, lq2_ref, lk2_ref, lam_init)
    key_chunk = lax.broadcasted_iota(jnp.int32, (tq, tq), 0) // CHUNK
    qry_chunk = lax.broadcasted_iota(jnp.int32, (tq, tq), 1) // CHUNK
    diag_mask = key_chunk <= qry_chunk

    acc_ref[...] = jnp.zeros_like(acc_ref)

    def absorb(j, stats, masked):
        ks = pl.multiple_of(j * tq, tq)
        new = []
        n_chains = 2 * n_heads

        def scores(chain):
            cols = slice(chain * dh, (chain + 1) * dh)
            return _dot_nt(k_ref[0, pl.ds(ks, tq), cols], q_ref[0, :, cols])

        pending = [scores(c) for c in range(SCORE_LOOKAHEAD)]
        ones = jnp.ones((SUM_ROWS, tq), BF16)
        for chain in range(n_chains):
            h = chain // 2
            s = pending.pop(0)
            if chain + SCORE_LOOKAHEAD < n_chains:
                pending.append(scores(chain + SCORE_LOOKAHEAD))
            if masked:
                s = jnp.where(diag_mask, s, NEG_INF)
            m_new = jnp.maximum(stats[chain], jnp.max(s, axis=0, keepdims=True))
            alpha = jnp.exp2(stats[chain] - m_new)
            p = jnp.exp2(s - m_new).astype(BF16)
            vt = jnp.concatenate([vt_ref[h * dv:(h + 1) * dv, pl.ds(ks, tq)], ones], axis=0)
            acc_ref[chain] = alpha * acc_ref[chain] + _dot(vt, p)
            new.append(m_new)
        return tuple(new)

    m0 = jnp.full((1, tq), NEG_INF, F32)
    stats = lax.fori_loop(0, i, functools.partial(absorb, masked=False), (m0,) * (2 * n_heads))
    absorb(i, stats, masked=True)

    for h in range(n_heads):
        l1, l2 = acc_ref[2 * h, dv:dv + 1, :], acc_ref[2 * h + 1, dv:dv + 1, :]
        o = acc_ref[2 * h, :dv, :] / l1 - lam * (acc_ref[2 * h + 1, :dv, :] / l2)
        on = o * lax.rsqrt(jnp.mean(o * o, axis=0, keepdims=True) + DA_SUBLN_EPS) * sub_ref[...]
        cat_ref[h * dv:(h + 1) * dv, :] = (on * (1.0 - lam_init)).astype(BF16)

    o_ref[0] = x_ref[0] + _dot_tn(cat_ref[...], wo_ref[...])


def _da_attn_prompt(lams, q, kb, vt, x, wo, subln, n_heads, lam_init):
    b, t, d = x.shape
    dh = d // (2 * n_heads)
    tq = ATTN_TILE if t % ATTN_TILE == 0 else t
    assert tq % CHUNK == 0
    blk = pl.BlockSpec((1, tq, d), lambda bi, i: (bi, i, 0))
    seq = pl.BlockSpec((1, t, d), lambda bi, i: (bi, 0, 0), pipeline_mode=pl.Buffered(1))
    seq_t = pl.BlockSpec((d, t), lambda bi, i: (0, bi), pipeline_mode=pl.Buffered(1))
    lam_spec = _resident((1, dh))
    return pl.pallas_call(
        functools.partial(_da_attn_kernel, n_heads=n_heads, dh=dh, tq=tq, lam_init=lam_init),
        out_shape=jax.ShapeDtypeStruct((b, t, d), F32),
        grid=(b, t // tq),
        in_specs=[lam_spec] * 4 + [blk, seq, seq_t, blk, _resident((d, d)),
                                   _resident((2 * dh, 1))],
        out_specs=blk,
        scratch_shapes=[pltpu.VMEM((2 * n_heads, 2 * dh + SUM_ROWS, tq), F32),
                        pltpu.VMEM((d, tq), BF16)],
        compiler_params=_params("parallel", "arbitrary"),
        name="da_attn_prompt",
    )(*lams, q, kb, vt, x, wo, subln.reshape(2 * dh, 1))


def _da_attn_sample_kernel(lq1_ref, lk1_ref, lq2_ref, lk2_ref, q_ref, kc_ref, vc_ref, kn_ref,
                           vn_ref, x_ref, wo_ref, sub_ref, o_ref, cat_ref, *, n_heads, dh,
                           lam_init):
    lam = _da_lambda(lq1_ref, lk1_ref, lq2_ref, lk2_ref, lam_init)
    rows_per_token = 2 * n_heads
    past = kc_ref.shape[1] // rows_per_token

    def cached(ref, row):
        return ref.at[0][pl.ds(row, past, stride=rows_per_token), :].astype(BF16)

    for h in range(n_heads):
        c0 = h * 2 * dh
        probs = []
        for c in range(2):
            qh = q_ref[0, :, c0 + c * dh:c0 + (c + 1) * dh]
            sc = _dot_nt(qh, cached(kc_ref, 2 * h + c))
            sn = _dot_nt(qh, kn_ref[0, :, c0 + c * dh:c0 + (c + 1) * dh])
            m = jnp.maximum(jnp.max(sc, axis=1, keepdims=True), jnp.max(sn, axis=1, keepdims=True))
            pc = jnp.exp2(sc - m)
            pn = jnp.exp2(sn - m)
            l = jnp.sum(pc, axis=1, keepdims=True) + jnp.sum(pn, axis=1, keepdims=True)
            probs.append((pc / l, pn / l))
        ac = (probs[0][0] - lam * probs[1][0]).astype(BF16)
        an = (probs[0][1] - lam * probs[1][1]).astype(BF16)
        vc = jnp.concatenate([cached(vc_ref, h), cached(vc_ref, n_heads + h)], axis=1)
        o = _dot(ac, vc) + _dot(an, vn_ref[0, :, c0:c0 + 2 * dh])
        cat_ref[:, c0:c0 + 2 * dh] = _da_head_out(o, sub_ref, lam_init)
    o_ref[0] = x_ref[0] + _dot(cat_ref[...], wo_ref[...])


def _da_attn_sample(lams, q, k_rows, v_rows, kb, vb, x, wo, subln, n_heads, lam_init):
    b, ts, d = x.shape
    dh = d // (2 * n_heads)
    past = k_rows.shape[1] // (2 * n_heads)
    assert past % CHUNK == 0 and ts <= CHUNK
    new = pl.BlockSpec((1, ts, d), lambda bi: (bi, 0, 0))
    cache = pl.BlockSpec((1, past * 2 * n_heads, dh), lambda bi: (bi, 0, 0))
    lam_spec = _resident((1, dh))
    return pl.pallas_call(
        functools.partial(_da_attn_sample_kernel, n_heads=n_heads, dh=dh, lam_init=lam_init),
        out_shape=jax.ShapeDtypeStruct((b, ts, d), F32),
        grid=(b,),
        in_specs=[lam_spec] * 4 + [new, cache, cache, new, new, new, _resident((d, d)),
                                   _resident((1, 2 * dh))],
        out_specs=new,
        scratch_shapes=[pltpu.VMEM((ts, d), BF16)],
        compiler_params=_params("parallel"),
        name="da_attn_sample",
    )(*lams, q, k_rows, v_rows, kb, vb, x, wo, subln.reshape(1, 2 * dh))


def _rope_tables(pos, dk):
    angle = 1.0 / (ROPE_BASE ** jnp.linspace(0.0, 1.0, dk // 2, dtype=F32))
    angle = jnp.repeat(angle, 2)
    theta = pos[:, None] * angle[None, :]
    sign = jnp.where(jnp.arange(dk) % 2 == 0, -1.0, 1.0).astype(F32)
    return jnp.cos(theta), jnp.sin(theta) * sign[None, :]


def _ret_proj_kernel(x_ref, g_ref, w_ref, cos_ref, sin_ref, q_ref, k_ref, v_ref, gate_ref, *,
                     n_heads, dk, k_scale):
    nq = n_heads * dk
    nv = v_ref.shape[1]
    h = _rmsnorm(x_ref[...], g_ref[...], NORM_EPS).astype(BF16)
    cos = cos_ref[...]
    sin = sin_ref[...]
    even = lax.broadcasted_iota(jnp.int32, cos.shape, 1) % 2 == 0

    def rotary(a):
        rot = jnp.where(even, pltpu.roll(a, dk - 1, 1), pltpu.roll(a, 1, 1))
        return a * cos + rot * sin

    q = _dot(h, w_ref[:, 0:nq])
    k = _dot(h, w_ref[:, nq:2 * nq])
    for hd in range(n_heads):
        sl = slice(hd * dk, (hd + 1) * dk)
        q_ref[:, sl] = rotary(q[:, sl]).astype(BF16)
        k_ref[:, sl] = (rotary(k[:, sl]) * k_scale).astype(BF16)
    v_ref[...] = _dot(h, w_ref[:, 2 * nq:2 * nq + nv]).astype(BF16)
    gate_ref[...] = _dot(h, w_ref[:, 2 * nq + nv:2 * nq + 2 * nv])


def _ret_proj(x, g, w, cos, sin, n_heads, dk, dv):
    n, d = x.shape
    nq, nv = n_heads * dk, n_heads * dv
    tm = _row_tile(cos.shape[0])
    pos_blocks = cos.shape[0] // tm
    row = lambda width: pl.BlockSpec((tm, width), lambda i: (i, 0))
    tab = pl.BlockSpec((tm, dk), lambda i: (i % pos_blocks, 0))
    return pl.pallas_call(
        functools.partial(_ret_proj_kernel, n_heads=n_heads, dk=dk, k_scale=dk ** -0.5),
        out_shape=(jax.ShapeDtypeStruct((n, nq), BF16), jax.ShapeDtypeStruct((n, nq), BF16),
                   jax.ShapeDtypeStruct((n, nv), BF16), jax.ShapeDtypeStruct((n, nv), F32)),
        grid=(n // tm,),
        in_specs=[row(d), _resident((1, d)), _resident((d, 2 * nq + 2 * nv)), tab, tab],
        out_specs=(row(nq), row(nq), row(nv), row(nv)),
        compiler_params=_params("parallel"),
        name="ret_proj",
    )(x, g.reshape(1, d), w, cos, sin)


def _ret_core_kernel(lg_ref, q_ref, k_ref, v_ref, gate_ref, x_ref, s0_ref, wo_ref, o_ref,
                     sfin_ref, s_scr, y_scr, *, n_heads, dk, dv, blk):
    c = pl.program_id(1)

    @pl.when(c == 0)
    def _():
        s_scr[...] = s0_ref[0]

    row = lax.broadcasted_iota(jnp.int32, (blk, blk), 0)
    col = lax.broadcasted_iota(jnp.int32, (blk, blk), 1)
    rel = (row - col).astype(F32)
    idx = lax.broadcasted_iota(jnp.int32, (blk, 1), 0).astype(F32)

    for h in range(n_heads):
        lg = lg_ref[h]
        dmask = jnp.where(rel >= 0, jnp.exp(jnp.maximum(rel, 0.0) * lg), 0.0)
        qh = q_ref[0, :, h * dk:(h + 1) * dk]
        kh = k_ref[0, :, h * dk:(h + 1) * dk]
        vh = v_ref[0, :, h * dv:(h + 1) * dv]
        s = s_scr[h]
        qk = _dot_nt(qh, kh) * dmask
        inner = _dot(qk.astype(BF16), vh)
        cross = _dot(qh, s.astype(BF16)) * jnp.exp((idx + 1.0) * lg)
        o = inner + cross
        kd = (kh.astype(F32) * jnp.exp((blk - 1.0 - idx) * lg)).astype(BF16)
        s_decay = jnp.exp(jnp.full((1, dv), float(blk), F32) * lg)
        s_scr[h] = s * s_decay + _dot_tn(kd, vh)
        on = o * lax.rsqrt(jnp.mean(o * o, axis=-1, keepdims=True) + RET_GN_EPS)
        gate = gate_ref[0, :, h * dv:(h + 1) * dv]
        y_scr[:, h * dv:(h + 1) * dv] = (gate * jax.nn.sigmoid(gate) * on).astype(BF16)

    o_ref[0] = x_ref[0] + _dot(y_scr[...], wo_ref[...])

    @pl.when(c == pl.num_programs(1) - 1)
    def _():
        sfin_ref[0] = s_scr[...]


def _ret_core(lg, q, k, v, gate, x, s0, wo, n_heads, dk, dv):
    b, t, d = x.shape
    blk = RET_TILE if t % RET_TILE == 0 else t
    nq, nv = n_heads * dk, n_heads * dv
    tok = lambda width: pl.BlockSpec((1, blk, width), lambda bi, c: (bi, c, 0))
    state = pl.BlockSpec((1, n_heads, dk, dv), lambda bi, c: (bi, 0, 0, 0))
    return pl.pallas_call(
        functools.partial(_ret_core_kernel, n_heads=n_heads, dk=dk, dv=dv, blk=blk),
        out_shape=(jax.ShapeDtypeStruct((b, t, d), F32),
                   jax.ShapeDtypeStruct((b, n_heads, dk, dv), F32)),
        grid=(b, t // blk),
        in_specs=[pl.BlockSpec(memory_space=pltpu.SMEM), tok(nq), tok(nq), tok(nv), tok(nv),
                  tok(d), state, _resident((nv, d))],
        out_specs=(tok(d), state),
        scratch_shapes=[pltpu.VMEM((n_heads, dk, dv), F32), pltpu.VMEM((blk, nv), BF16)],
        compiler_params=_params("parallel", "arbitrary"),
        name="ret_core",
    )(lg, q, k, v, gate, x, s0, wo)


def _lambda_init(layer_idx):
    return 0.8 - 0.6 * math.exp(-0.3 * layer_idx)


def kernel(x_prompt, x_sample, cache_diff_k, cache_diff_v, state_ret, ffn1_norm, ffn1_w_gate, ffn1_w_up, ffn1_w_down, mix_norm, da_w_qkv, da_lambda_q1, da_lambda_k1, da_lambda_q2, da_lambda_k2, da_subln, da_w_o, ret_w_in, ret_w_o, ffn2_norm, ffn2_w_gate, ffn2_w_up, ffn2_w_down, final_norm):
    bp, tp, d = x_prompt.shape
    bs, ts, _ = x_sample.shape
    past = cache_diff_k.shape[2]
    da_heads = cache_diff_k.shape[3]
    dh = cache_diff_k.shape[5]
    ret_heads, dk, dv = state_ret.shape[2:]
    depth = ffn1_norm.shape[0]
    bf = lambda w: w.astype(BF16)

    xp = x_prompt.reshape(bp * tp, d)
    xs = x_sample.reshape(bs * ts, d)
    kp_list, vp_list, sp_list = [], [], []
    ks_list, vs_list, ss_list = [], [], []
    for i in range(depth):
        xp, *w1 = _ffn(xp, ffn1_norm[i], ffn1_w_gate, ffn1_w_up, ffn1_w_down, layer=i)
        xs = _ffn(xs, ffn1_norm[i], *w1)
        if i % 2 == 0:
            a = i // 2
            lam_init = _lambda_init(i)
            w_qkv, w_o = bf(da_w_qkv[a]), bf(da_w_o[a])
            lams = [v[a].reshape(1, dh) for v in (da_lambda_q1, da_lambda_k1, da_lambda_q2, da_lambda_k2)]
            w_vt = bf(da_w_qkv[a][:, 2 * d:].T)
            q_scale = dh ** -0.5 * math.log2(math.e)
            qp, kp, vp, kpb, vpt = _da_proj(xp, mix_norm[i], w_qkv, w_vt, q_scale, True, da_heads)
            qs, ks, vs, ksb, vsb = _da_proj(xs, mix_norm[i], w_qkv, w_vt, q_scale, False, da_heads)
            seq = lambda z, b, t: z.reshape(b, t, d)
            xp = _da_attn_prompt(lams, seq(qp, bp, tp), seq(kpb, bp, tp), vpt,
                                 seq(xp, bp, tp), w_o, da_subln[a], da_heads,
                                 lam_init).reshape(bp * tp, d)
            xs = _da_attn_sample(lams, seq(qs, bs, ts), _key_cache_to_rows(cache_diff_k[a]),
                                 _value_cache_to_rows(cache_diff_v[a]), seq(ksb, bs, ts),
                                 seq(vsb, bs, ts), seq(xs, bs, ts), w_o, da_subln[a], da_heads,
                                 lam_init).reshape(bs * ts, d)
            kp_list.append(_key_rows_to_cache(kp, bp, tp, da_heads, dh))
            vp_list.append(_value_rows_to_cache(vp, bp, tp, da_heads, dh))
            ks_list.append(_key_rows_to_cache(ks, bs, ts, da_heads, dh))
            vs_list.append(_value_rows_to_cache(vs, bs, ts, da_heads, dh))
        else:
            r = i // 2
            w_in, w_o = bf(ret_w_in[r]), bf(ret_w_o[r])
            lg = jnp.log1p(-jnp.exp2(-5.0 - jnp.arange(ret_heads, dtype=F32)))
            nq, nv = ret_heads * dk, ret_heads * dv
            groups = ((xp, bp, tp, jnp.arange(tp, dtype=F32),
                       jnp.zeros((bp, ret_heads, dk, dv), F32), sp_list),
                      (xs, bs, ts, past + jnp.arange(ts, dtype=F32),
                       state_ret[r].astype(F32), ss_list))
            outs = []
            for x, b, t, pos, s0, s_list in groups:
                cos, sin = _rope_tables(pos, dk)
                if t % ROW_TILE != 0:
                    cos, sin = jnp.tile(cos, (b, 1)), jnp.tile(sin, (b, 1))
                q, k, v, gate = _ret_proj(x, mix_norm[i], w_in, cos, sin, ret_heads, dk, dv)
                y, s_fin = _ret_core(lg, q.reshape(b, t, nq), k.reshape(b, t, nq),
                                     v.reshape(b, t, nv), gate.reshape(b, t, nv),
                                     x.reshape(b, t, d), s0, w_o, ret_heads, dk, dv)
                outs.append(y.reshape(b * t, d))
                s_list.append(s_fin)
            xp, xs = outs
        fin = final_norm if i == depth - 1 else None
        xp, *w2 = _ffn(xp, ffn2_norm[i], ffn2_w_gate, ffn2_w_up, ffn2_w_down, final_g=fin, layer=i)
        xs = _ffn(xs, ffn2_norm[i], *w2, final_g=fin)

    return (xp.reshape(bp, tp, d), xs.reshape(bs, ts, d),
            jnp.stack(kp_list), jnp.stack(vp_list), jnp.stack(sp_list),
            jnp.stack(ks_list), jnp.stack(vs_list), jnp.stack(ss_list))
```

```python
import functools
import math

import jax
import jax.numpy as jnp
from jax import lax
from jax.experimental import pallas as pl
from jax.experimental.pallas import tpu as pltpu

F32 = jnp.float32
BF16 = jnp.bfloat16

NORM_EPS = 1e-6
DA_SUBLN_EPS = 1e-5
RET_GN_EPS = 1e-6
ROPE_BASE = 10000.0
CHUNK = 64
NEG_INF = -1e30

LANES = 128
V7X_VMEM_BYTES = 64 * 1024 * 1024
VMEM_LIMIT_BYTES = (V7X_VMEM_BYTES * 3) // 4

ROW_TILE = 512
ATTN_TILE = 512
RET_TILE = 256
SCORE_LOOKAHEAD = 2
WEIGHT_STAGE_CHUNKS = 8
SUM_ROWS = 16


def _params(*semantics):
    return pltpu.CompilerParams(dimension_semantics=semantics,
                                vmem_limit_bytes=VMEM_LIMIT_BYTES)


def _resident(shape):
    return pl.BlockSpec(shape, lambda *_: (0,) * len(shape),
                        pipeline_mode=pl.Buffered(1))


def _rmsnorm(x, g, eps):
    return x * lax.rsqrt(jnp.mean(x * x, axis=-1, keepdims=True) + eps) * g


def _dot(a, b):
    return jnp.dot(a, b, preferred_element_type=F32)


def _dot_nt(a, b):
    return lax.dot_general(a, b, (((1,), (1,)), ((), ())), preferred_element_type=F32)


def _dot_tn(a, b):
    return lax.dot_general(a, b, (((0,), (0,)), ((), ())), preferred_element_type=F32)


def _row_tile(n):
    return ROW_TILE if n % ROW_TILE == 0 else n


def _stage_and_cast(src_hbm, dst_vmem, stage, sem):
    rows = stage.shape[1]
    n_chunks = src_hbm.shape[0] // rows

    def chunk_copy(c, slot):
        return pltpu.make_async_copy(src_hbm.at[pl.ds(c * rows, rows)], stage.at[slot], sem.at[slot])

    chunk_copy(0, 0).start()

    def body(c, carry):
        slot = lax.rem(c, 2)

        @pl.when(c + 1 < n_chunks)
        def _():
            chunk_copy(c + 1, 1 - slot).start()

        chunk_copy(c, slot).wait()
        dst_vmem[pl.ds(pl.multiple_of(c * rows, rows), rows), :] = stage[slot].astype(BF16)
        return carry

    lax.fori_loop(0, n_chunks, body, 0)


def _stage_shape(weight_shape):
    rows, cols = weight_shape
    assert rows % WEIGHT_STAGE_CHUNKS == 0
    return (2, rows // WEIGHT_STAGE_CHUNKS, cols)


def _ffn_body(x_ref, g_ref, wg_ref, wu_ref, wd_ref, fg_ref, o_ref, f_tile):
    x = x_ref[...]
    h = _rmsnorm(x, g_ref[...], NORM_EPS).astype(BF16)
    acc = None
    for c in range(wg_ref.shape[1] // f_tile):
        sl = slice(c * f_tile, (c + 1) * f_tile)
        gate = _dot(h, wg_ref[:, sl])
        up = _dot(h, wu_ref[:, sl])
        a = (gate * jax.nn.sigmoid(gate) * up).astype(BF16)
        d = _dot(a, wd_ref[sl, :])
        acc = d if acc is None else acc + d
    y = x + 0.5 * acc
    if fg_ref is not None:
        y = _rmsnorm(y, fg_ref[...], NORM_EPS)
    o_ref[...] = y


def _ffn_kernel(x_ref, g_ref, wg_ref, wu_ref, wd_ref, *rest, f_tile, final):
    fg_ref, o_ref = rest if final else (None,) + rest
    _ffn_body(x_ref, g_ref, wg_ref, wu_ref, wd_ref, fg_ref, o_ref, f_tile)


def _ffn_cast_kernel(x_ref, g_ref, wg_hbm, wu_hbm, wd_hbm, *rest, f_tile, final, layer):
    fg_ref = rest[0] if final else None
    o_ref, wg_out, wu_out, wd_out, wg_v, wu_v, wd_v, stage_in, stage_down, sem_in, sem_out = (
        rest[1:] if final else rest)
    i = pl.program_id(0)
    copies_out = [pltpu.make_async_copy(v, out, sem_out.at[k])
                  for k, (v, out) in enumerate(((wg_v, wg_out), (wu_v, wu_out), (wd_v, wd_out)))]

    @pl.when(i == 0)
    def _():
        for src, dst, stage, copy_out in ((wg_hbm, wg_v, stage_in, copies_out[0]),
                                          (wu_hbm, wu_v, stage_in, copies_out[1]),
                                          (wd_hbm, wd_v, stage_down, copies_out[2])):
            _stage_and_cast(src.at[layer], dst, stage, sem_in)
            copy_out.start()

    _ffn_body(x_ref, g_ref, wg_v, wu_v, wd_v, fg_ref, o_ref, f_tile)

    @pl.when(i == pl.num_programs(0) - 1)
    def _():
        for copy_out in copies_out:
            copy_out.wait()


def _ffn(x, g, wg, wu, wd, final_g=None, layer=None):
    n, d = x.shape
    f = wg.shape[-1]
    tm = _row_tile(n)
    f_tile = 256 if f % 256 == 0 else f
    final = final_g is not None
    cast = layer is not None
    row = pl.BlockSpec((tm, d), lambda i: (i, 0))
    hbm = pl.BlockSpec(memory_space=pl.ANY)
    w_specs = [hbm] * 3 if cast else [_resident((d, f)), _resident((d, f)), _resident((f, d))]
    in_specs = [row, _resident((1, d))] + w_specs
    args = [x, g.reshape(1, d), wg, wu, wd]
    if final:
        in_specs.append(_resident((1, d)))
        args.append(final_g.reshape(1, d))
    y_shape = jax.ShapeDtypeStruct((n, d), F32)
    name = "ffn_final" if final else "ffn"
    if not cast:
        return pl.pallas_call(
            functools.partial(_ffn_kernel, f_tile=f_tile, final=final),
            out_shape=y_shape, grid=(n // tm,), in_specs=in_specs, out_specs=row,
            compiler_params=_params("parallel"), name=name,
        )(*args)
    up_shape, down_shape = wg.shape[1:], wd.shape[1:]
    bf16_like = lambda shape: jax.ShapeDtypeStruct(shape, BF16)
    return pl.pallas_call(
        functools.partial(_ffn_cast_kernel, f_tile=f_tile, final=final, layer=layer),
        out_shape=(y_shape, bf16_like(up_shape), bf16_like(up_shape), bf16_like(down_shape)),
        grid=(n // tm,),
        in_specs=in_specs,
        out_specs=(row, hbm, hbm, hbm),
        scratch_shapes=[pltpu.VMEM(up_shape, BF16), pltpu.VMEM(up_shape, BF16),
                        pltpu.VMEM(down_shape, BF16), pltpu.VMEM(_stage_shape(up_shape), F32),
                        pltpu.VMEM(_stage_shape(down_shape), F32),
                        pltpu.SemaphoreType.DMA((2,)), pltpu.SemaphoreType.DMA((3,))],
        compiler_params=_params("arbitrary"),
        name=name + "_cast",
    )(*args)


def _da_proj_body(x_ref, g_ref, w_ref, q_ref, k_ref, v_ref, kb_ref, vb_ref, q_scale,
                  transposed_v, n_heads):
    tm, d = x_ref.shape
    lane_tiles = d // LANES
    h = _rmsnorm(x_ref[...], g_ref[...], NORM_EPS).astype(BF16)
    q_ref[...] = (_dot(h, w_ref[:, 0:d]) * q_scale).astype(BF16)
    k = _dot(h, w_ref[:, d:2 * d])
    kb_ref[...] = k.astype(BF16)
    v = _dot(h, w_ref[:, 2 * d:3 * d])
    for j in range(lane_tiles):
        cols = slice(j * LANES, (j + 1) * LANES)
        k_ref[pl.ds(j, tm, stride=lane_tiles), :] = k[:, cols]
        head, half = divmod(j, lane_tiles // n_heads)
        v_ref[pl.ds(half * n_heads + head, tm, stride=lane_tiles), :] = v[:, cols]
    vb_ref[...] = (v.T if transposed_v else v).astype(BF16)


def _da_proj_kernel(x_ref, g_ref, w_ref, *outs, **static):
    _da_proj_body(x_ref, g_ref, w_ref, *outs, **static)


def _da_proj_cast_kernel(x_ref, g_ref, w_hbm, q_ref, k_ref, v_ref, kb_ref, vb_ref, w_out, w_v,
                         stage, sem_in, sem_out, *, layer, **static):
    i = pl.program_id(0)
    copy_out = pltpu.make_async_copy(w_v, w_out, sem_out.at[0])

    @pl.when(i == 0)
    def _():
        _stage_and_cast(w_hbm.at[layer], w_v, stage, sem_in)
        copy_out.start()

    _da_proj_body(x_ref, g_ref, w_v, q_ref, k_ref, v_ref, kb_ref, vb_ref, **static)

    @pl.when(i == pl.num_programs(0) - 1)
    def _():
        copy_out.wait()


def _da_proj(x, g, w, q_scale, transposed_v, n_heads, layer=None):
    n, d = x.shape
    tm = _row_tile(n)
    lane_tiles = d // LANES
    cast = layer is not None
    row = pl.BlockSpec((tm, d), lambda i: (i, 0))
    rows = pl.BlockSpec((tm * lane_tiles, LANES), lambda i: (i, 0))
    hbm = pl.BlockSpec(memory_space=pl.ANY)
    if transposed_v:
        vb_shape, vb_spec = (d, n), pl.BlockSpec((d, tm), lambda i: (0, i))
    else:
        vb_shape, vb_spec = (n, d), row
    out_shape = [jax.ShapeDtypeStruct((n, d), BF16),
                 jax.ShapeDtypeStruct((n * lane_tiles, LANES), F32),
                 jax.ShapeDtypeStruct((n * lane_tiles, LANES), F32),
                 jax.ShapeDtypeStruct((n, d), BF16),
                 jax.ShapeDtypeStruct(vb_shape, BF16)]
    out_specs = [row, rows, rows, row, vb_spec]
    static = dict(q_scale=q_scale, transposed_v=transposed_v, n_heads=n_heads)
    name = "da_proj_t" if transposed_v else "da_proj"
    if not cast:
        return pl.pallas_call(
            functools.partial(_da_proj_kernel, **static),
            out_shape=out_shape, grid=(n // tm,),
            in_specs=[row, _resident((1, d)), _resident((d, 3 * d))], out_specs=out_specs,
            compiler_params=_params("parallel"), name=name,
        )(x, g.reshape(1, d), w)
    w_shape = w.shape[1:]
    return pl.pallas_call(
        functools.partial(_da_proj_cast_kernel, layer=layer, **static),
        out_shape=out_shape + [jax.ShapeDtypeStruct(w_shape, BF16)], grid=(n // tm,),
        in_specs=[row, _resident((1, d)), hbm], out_specs=out_specs + [hbm],
        scratch_shapes=[pltpu.VMEM(w_shape, BF16), pltpu.VMEM(_stage_shape(w_shape), F32),
                        pltpu.SemaphoreType.DMA((2,)), pltpu.SemaphoreType.DMA((1,))],
        compiler_params=_params("arbitrary"), name=name + "_cast",
    )(x, g.reshape(1, d), w)


def _key_rows_to_cache(k_rows, b, t, n_heads, dh):
    return k_rows.reshape(b, t, n_heads, 2, dh)


def _value_rows_to_cache(v_rows, b, t, n_heads, dh):
    v = v_rows.reshape(b, t, 2, n_heads, dh)
    return jnp.swapaxes(v, 2, 3).reshape(b, t, n_heads, 2 * dh)


def _key_cache_to_rows(k_cache):
    b, t, n_heads, _, dh = k_cache.shape
    return k_cache.reshape(b, t * n_heads * 2, dh)


def _value_cache_to_rows(v_cache):
    b, t, n_heads, dv = v_cache.shape
    v = v_cache.reshape(b, t, n_heads, 2, dv // 2)
    return jnp.swapaxes(v, 2, 3).reshape(b, t * 2 * n_heads, dv // 2)


def _da_lambda(lq1_ref, lk1_ref, lq2_ref, lk2_ref, lam_init):
    s1 = jnp.sum(lq1_ref[...] * lk1_ref[...], axis=1, keepdims=True)
    s2 = jnp.sum(lq2_ref[...] * lk2_ref[...], axis=1, keepdims=True)
    return jnp.exp(s1) - jnp.exp(s2) + lam_init


def _da_head_out(o, sub_ref, lam_init):
    return (_rmsnorm(o, sub_ref[...], DA_SUBLN_EPS) * (1.0 - lam_init)).astype(BF16)


def _da_attn_kernel(lq1_ref, lk1_ref, lq2_ref, lk2_ref, q_ref, k_ref, vt_ref, x_ref, wo_ref,
                    sub_ref, o_ref, acc_ref, cat_ref, *, n_heads, dh, tq, lam_init):
    i = pl.program_id(1)
    dv = 2 * dh
    lam = _da_lambda(lq1_ref, lk1_ref, lq2_ref, lk2_ref, lam_init)
    key_chunk = lax.broadcasted_iota(jnp.int32, (tq, tq), 0) // CHUNK
    qry_chunk = lax.broadcasted_iota(jnp.int32, (tq, tq), 1) // CHUNK
    diag_mask = key_chunk <= qry_chunk

    acc_ref[...] = jnp.zeros_like(acc_ref)

    def absorb(j, stats, masked):
        ks = pl.multiple_of(j * tq, tq)
        new = []
        n_chains = 2 * n_heads

        def scores(chain):
            cols = slice(chain * dh, (chain + 1) * dh)
            return _dot_nt(k_ref[0, pl.ds(ks, tq), cols], q_ref[0, :, cols])

        pending = [scores(c) for c in range(SCORE_LOOKAHEAD)]
        ones = jnp.ones((SUM_ROWS, tq), BF16)
        for chain in range(n_chains):
            h = chain // 2
            s = pending.pop(0)
            if chain + SCORE_LOOKAHEAD < n_chains:
                pending.append(scores(chain + SCORE_LOOKAHEAD))
            if masked:
                s = jnp.where(diag_mask, s, NEG_INF)
            m_new = jnp.maximum(stats[chain], jnp.max(s, axis=0, keepdims=True))
            alpha = jnp.exp2(stats[chain] - m_new)
            p = jnp.exp2(s - m_new).astype(BF16)
            vt = jnp.concatenate([vt_ref[h * dv:(h + 1) * dv, pl.ds(ks, tq)], ones], axis=0)
            acc_ref[chain] = alpha * acc_ref[chain] + _dot(vt, p)
            new.append(m_new)
        return tuple(new)

    m0 = jnp.full((1, tq), NEG_INF, F32)
    stats = lax.fori_loop(0, i, functools.partial(absorb, masked=False), (m0,) * (2 * n_heads))
    absorb(i, stats, masked=True)

    for h in range(n_heads):
        l1, l2 = acc_ref[2 * h, dv:dv + 1, :], acc_ref[2 * h + 1, dv:dv + 1, :]
        o = acc_ref[2 * h, :dv, :] / l1 - lam * (acc_ref[2 * h + 1, :dv, :] / l2)
        on = o * lax.rsqrt(jnp.mean(o * o, axis=0, keepdims=True) + DA_SUBLN_EPS) * sub_ref[...]
        cat_ref[h * dv:(h + 1) * dv, :] = (on * (1.0 - lam_init)).astype(BF16)

    o_ref[0] = x_ref[0] + _dot_tn(cat_ref[...], wo_ref[...])


def _da_attn_prompt(lams, q, kb, vt, x, wo, subln, n_heads, lam_init):
    b, t, d = x.shape
    dh = d // (2 * n_heads)
    tq = ATTN_TILE if t % ATTN_TILE == 0 else t
    assert tq % CHUNK == 0
    blk = pl.BlockSpec((1, tq, d), lambda bi, i: (bi, i, 0))
    seq = pl.BlockSpec((1, t, d), lambda bi, i: (bi, 0, 0), pipeline_mode=pl.Buffered(1))
    seq_t = pl.BlockSpec((d, t), lambda bi, i: (0, bi), pipeline_mode=pl.Buffered(1))
    lam_spec = _resident((1, dh))
    return pl.pallas_call(
        functools.partial(_da_attn_kernel, n_heads=n_heads, dh=dh, tq=tq, lam_init=lam_init),
        out_shape=jax.ShapeDtypeStruct((b, t, d), F32),
        grid=(b, t // tq),
        in_specs=[lam_spec] * 4 + [blk, seq, seq_t, blk, _resident((d, d)),
                                   _resident((2 * dh, 1))],
        out_specs=blk,
        scratch_shapes=[pltpu.VMEM((2 * n_heads, 2 * dh + SUM_ROWS, tq), F32),
                        pltpu.VMEM((d, tq), BF16)],
        compiler_params=_params("parallel", "arbitrary"),
        name="da_attn_prompt",
    )(*lams, q, kb, vt, x, wo, subln.reshape(2 * dh, 1))


def _da_attn_sample_kernel(lq1_ref, lk1_ref, lq2_ref, lk2_ref, q_ref, kc_ref, vc_ref, kn_ref,
                           vn_ref, x_ref, wo_ref, sub_ref, o_ref, cat_ref, *, n_heads, dh,
                           lam_init):
    lam = _da_lambda(lq1_ref, lk1_ref, lq2_ref, lk2_ref, lam_init)
    rows_per_token = 2 * n_heads
    past = kc_ref.shape[1] // rows_per_token

    def cached(ref, row):
        return ref.at[0][pl.ds(row, past, stride=rows_per_token), :].astype(BF16)

    for h in range(n_heads):
        c0 = h * 2 * dh
        probs = []
        for c in range(2):
            qh = q_ref[0, :, c0 + c * dh:c0 + (c + 1) * dh]
            sc = _dot_nt(qh, cached(kc_ref, 2 * h + c))
            sn = _dot_nt(qh, kn_ref[0, :, c0 + c * dh:c0 + (c + 1) * dh])
            m = jnp.maximum(jnp.max(sc, axis=1, keepdims=True), jnp.max(sn, axis=1, keepdims=True))
            pc = jnp.exp2(sc - m)
            pn = jnp.exp2(sn - m)
            l = jnp.sum(pc, axis=1, keepdims=True) + jnp.sum(pn, axis=1, keepdims=True)
            probs.append((pc / l, pn / l))
        ac = (probs[0][0] - lam * probs[1][0]).astype(BF16)
        an = (probs[0][1] - lam * probs[1][1]).astype(BF16)
        vc = jnp.concatenate([cached(vc_ref, h), cached(vc_ref, n_heads + h)], axis=1)
        o = _dot(ac, vc) + _dot(an, vn_ref[0, :, c0:c0 + 2 * dh])
        cat_ref[:, c0:c0 + 2 * dh] = _da_head_out(o, sub_ref, lam_init)
    o_ref[0] = x_ref[0] + _dot(cat_ref[...], wo_ref[...])


def _da_attn_sample(lams, q, k_rows, v_rows, kb, vb, x, wo, subln, n_heads, lam_init):
    b, ts, d = x.shape
    dh = d // (2 * n_heads)
    past = k_rows.shape[1] // (2 * n_heads)
    assert past % CHUNK == 0 and ts <= CHUNK
    new = pl.BlockSpec((1, ts, d), lambda bi: (bi, 0, 0))
    cache = pl.BlockSpec((1, past * 2 * n_heads, dh), lambda bi: (bi, 0, 0))
    lam_spec = _resident((1, dh))
    return pl.pallas_call(
        functools.partial(_da_attn_sample_kernel, n_heads=n_heads, dh=dh, lam_init=lam_init),
        out_shape=jax.ShapeDtypeStruct((b, ts, d), F32),
        grid=(b,),
        in_specs=[lam_spec] * 4 + [new, cache, cache, new, new, new, _resident((d, d)),
                                   _resident((1, 2 * dh))],
        out_specs=new,
        scratch_shapes=[pltpu.VMEM((ts, d), BF16)],
        compiler_params=_params("parallel"),
        name="da_attn_sample",
    )(*lams, q, k_rows, v_rows, kb, vb, x, wo, subln.reshape(1, 2 * dh))


def _rope_tables(pos, dk):
    angle = 1.0 / (ROPE_BASE ** jnp.linspace(0.0, 1.0, dk // 2, dtype=F32))
    angle = jnp.repeat(angle, 2)
    theta = pos[:, None] * angle[None, :]
    sign = jnp.where(jnp.arange(dk) % 2 == 0, -1.0, 1.0).astype(F32)
    return jnp.cos(theta), jnp.sin(theta) * sign[None, :]


def _ret_proj_kernel(x_ref, g_ref, w_ref, cos_ref, sin_ref, q_ref, k_ref, v_ref, gate_ref, *,
                     n_heads, dk, k_scale):
    nq = n_heads * dk
    nv = v_ref.shape[1]
    h = _rmsnorm(x_ref[...], g_ref[...], NORM_EPS).astype(BF16)
    cos = cos_ref[...]
    sin = sin_ref[...]
    even = lax.broadcasted_iota(jnp.int32, cos.shape, 1) % 2 == 0

    def rotary(a):
        rot = jnp.where(even, pltpu.roll(a, dk - 1, 1), pltpu.roll(a, 1, 1))
        return a * cos + rot * sin

    q = _dot(h, w_ref[:, 0:nq])
    k = _dot(h, w_ref[:, nq:2 * nq])
    for hd in range(n_heads):
        sl = slice(hd * dk, (hd + 1) * dk)
        q_ref[:, sl] = rotary(q[:, sl]).astype(BF16)
        k_ref[:, sl] = (rotary(k[:, sl]) * k_scale).astype(BF16)
    v_ref[...] = _dot(h, w_ref[:, 2 * nq:2 * nq + nv]).astype(BF16)
    gate_ref[...] = _dot(h, w_ref[:, 2 * nq + nv:2 * nq + 2 * nv])


def _ret_proj(x, g, w, cos, sin, n_heads, dk, dv):
    n, d = x.shape
    nq, nv = n_heads * dk, n_heads * dv
    tm = _row_tile(cos.shape[0])
    pos_blocks = cos.shape[0] // tm
    row = lambda width: pl.BlockSpec((tm, width), lambda i: (i, 0))
    tab = pl.BlockSpec((tm, dk), lambda i: (i % pos_blocks, 0))
    return pl.pallas_call(
        functools.partial(_ret_proj_kernel, n_heads=n_heads, dk=dk, k_scale=dk ** -0.5),
        out_shape=(jax.ShapeDtypeStruct((n, nq), BF16), jax.ShapeDtypeStruct((n, nq), BF16),
                   jax.ShapeDtypeStruct((n, nv), BF16), jax.ShapeDtypeStruct((n, nv), F32)),
        grid=(n // tm,),
        in_specs=[row(d), _resident((1, d)), _resident((d, 2 * nq + 2 * nv)), tab, tab],
        out_specs=(row(nq), row(nq), row(nv), row(nv)),
        compiler_params=_params("parallel"),
        name="ret_proj",
    )(x, g.reshape(1, d), w, cos, sin)


def _ret_core_kernel(lg_ref, q_ref, k_ref, v_ref, gate_ref, x_ref, s0_ref, wo_ref, o_ref,
                     sfin_ref, s_scr, y_scr, decay_scr, *, n_heads, dk, dv, blk):
    c = pl.program_id(1)

    row = lax.broadcasted_iota(jnp.int32, (blk, blk), 0)
    col = lax.broadcasted_iota(jnp.int32, (blk, blk), 1)
    rel = (row - col).astype(F32)
    idx = lax.broadcasted_iota(jnp.int32, (blk, 1), 0).astype(F32)

    @pl.when(c == 0)
    def _():
        s_scr[...] = s0_ref[0]
        for h in range(n_heads):
            decay_scr[h] = jnp.where(rel >= 0, jnp.exp(jnp.maximum(rel, 0.0) * lg_ref[h]), 0.0)

    first = []
    for h in range(n_heads):
        lg = lg_ref[h]
        qh = q_ref[0, :, h * dk:(h + 1) * dk]
        kh = k_ref[0, :, h * dk:(h + 1) * dk]
        vh = v_ref[0, :, h * dv:(h + 1) * dv]
        s = s_scr[h]
        qk = _dot_nt(qh, kh)
        cross = _dot(qh, s.astype(BF16))
        kd = (kh.astype(F32) * jnp.exp((blk - 1.0 - idx) * lg)).astype(BF16)
        s_decay = jnp.exp(jnp.full((1, dv), float(blk), F32) * lg)
        s_scr[h] = s * s_decay + _dot_tn(kd, vh)
        first.append((qk, cross))

    for h, (qk, cross) in enumerate(first):
        vh = v_ref[0, :, h * dv:(h + 1) * dv]
        inner = _dot((qk * decay_scr[h]).astype(BF16), vh)
        o = inner + cross * jnp.exp((idx + 1.0) * lg_ref[h])
        on = o * lax.rsqrt(jnp.mean(o * o, axis=-1, keepdims=True) + RET_GN_EPS)
        gate = gate_ref[0, :, h * dv:(h + 1) * dv]
        y_scr[:, h * dv:(h + 1) * dv] = (gate * jax.nn.sigmoid(gate) * on).astype(BF16)

    o_ref[0] = x_ref[0] + _dot(y_scr[...], wo_ref[...])

    @pl.when(c == pl.num_programs(1) - 1)
    def _():
        sfin_ref[0] = s_scr[...]


def _ret_core(lg, q, k, v, gate, x, s0, wo, n_heads, dk, dv):
    b, t, d = x.shape
    blk = RET_TILE if t % RET_TILE == 0 else t
    nq, nv = n_heads * dk, n_heads * dv
    tok = lambda width: pl.BlockSpec((1, blk, width), lambda bi, c: (bi, c, 0))
    state = pl.BlockSpec((1, n_heads, dk, dv), lambda bi, c: (bi, 0, 0, 0))
    return pl.pallas_call(
        functools.partial(_ret_core_kernel, n_heads=n_heads, dk=dk, dv=dv, blk=blk),
        out_shape=(jax.ShapeDtypeStruct((b, t, d), F32),
                   jax.ShapeDtypeStruct((b, n_heads, dk, dv), F32)),
        grid=(b, t // blk),
        in_specs=[pl.BlockSpec(memory_space=pltpu.SMEM), tok(nq), tok(nq), tok(nv), tok(nv),
                  tok(d), state, _resident((nv, d))],
        out_specs=(tok(d), state),
        scratch_shapes=[pltpu.VMEM((n_heads, dk, dv), F32), pltpu.VMEM((blk, nv), BF16),
                        pltpu.VMEM((n_heads, blk, blk), F32)],
        compiler_params=_params("parallel", "arbitrary"),
        name="ret_core",
    )(lg, q, k, v, gate, x, s0, wo)


def _lambda_init(layer_idx):
    return 0.8 - 0.6 * math.exp(-0.3 * layer_idx)


def kernel(x_prompt, x_sample, cache_diff_k, cache_diff_v, state_ret, ffn1_norm, ffn1_w_gate, ffn1_w_up, ffn1_w_down, mix_norm, da_w_qkv, da_lambda_q1, da_lambda_k1, da_lambda_q2, da_lambda_k2, da_subln, da_w_o, ret_w_in, ret_w_o, ffn2_norm, ffn2_w_gate, ffn2_w_up, ffn2_w_down, final_norm):
    bp, tp, d = x_prompt.shape
    bs, ts, _ = x_sample.shape
    past = cache_diff_k.shape[2]
    da_heads = cache_diff_k.shape[3]
    dh = cache_diff_k.shape[5]
    ret_heads, dk, dv = state_ret.shape[2:]
    depth = ffn1_norm.shape[0]
    bf = lambda w: w.astype(BF16)

    xp = x_prompt.reshape(bp * tp, d)
    xs = x_sample.reshape(bs * ts, d)
    kp_list, vp_list, sp_list = [], [], []
    ks_list, vs_list, ss_list = [], [], []
    for i in range(depth):
        xp, *w1 = _ffn(xp, ffn1_norm[i], ffn1_w_gate, ffn1_w_up, ffn1_w_down, layer=i)
        xs = _ffn(xs, ffn1_norm[i], *w1)
        if i % 2 == 0:
            a = i // 2
            lam_init = _lambda_init(i)
            w_o = bf(da_w_o[a])
            lams = [v[a].reshape(1, dh) for v in (da_lambda_q1, da_lambda_k1, da_lambda_q2, da_lambda_k2)]
            q_scale = dh ** -0.5 * math.log2(math.e)
            qp, kp, vp, kpb, vpt, w_qkv = _da_proj(xp, mix_norm[i], da_w_qkv, q_scale, True,
                                                   da_heads, layer=a)
            qs, ks, vs, ksb, vsb = _da_proj(xs, mix_norm[i], w_qkv, q_scale, False, da_heads)
            seq = lambda z, b, t: z.reshape(b, t, d)
            xp = _da_attn_prompt(lams, seq(qp, bp, tp), seq(kpb, bp, tp), vpt,
                                 seq(xp, bp, tp), w_o, da_subln[a], da_heads,
                                 lam_init).reshape(bp * tp, d)
            xs = _da_attn_sample(lams, seq(qs, bs, ts), _key_cache_to_rows(cache_diff_k[a]),
                                 _value_cache_to_rows(cache_diff_v[a]), seq(ksb, bs, ts),
                                 seq(vsb, bs, ts), seq(xs, bs, ts), w_o, da_subln[a], da_heads,
                                 lam_init).reshape(bs * ts, d)
            kp_list.append(_key_rows_to_cache(kp, bp, tp, da_heads, dh))
            vp_list.append(_value_rows_to_cache(vp, bp, tp, da_heads, dh))
            ks_list.append(_key_rows_to_cache(ks, bs, ts, da_heads, dh))
            vs_list.append(_value_rows_to_cache(vs, bs, ts, da_heads, dh))
        else:
            r = i // 2
            w_in, w_o = bf(ret_w_in[r]), bf(ret_w_o[r])
            lg = jnp.log1p(-jnp.exp2(-5.0 - jnp.arange(ret_heads, dtype=F32)))
            nq, nv = ret_heads * dk, ret_heads * dv
            groups = ((xp, bp, tp, jnp.arange(tp, dtype=F32),
                       jnp.zeros((bp, ret_heads, dk, dv), F32), sp_list),
                      (xs, bs, ts, past + jnp.arange(ts, dtype=F32),
                       state_ret[r].astype(F32), ss_list))
            outs = []
            for x, b, t, pos, s0, s_list in groups:
                cos, sin = _rope_tables(pos, dk)
                if t % ROW_TILE != 0:
                    cos, sin = jnp.tile(cos, (b, 1)), jnp.tile(sin, (b, 1))
                q, k, v, gate = _ret_proj(x, mix_norm[i], w_in, cos, sin, ret_heads, dk, dv)
                y, s_fin = _ret_core(lg, q.reshape(b, t, nq), k.reshape(b, t, nq),
                                     v.reshape(b, t, nv), gate.reshape(b, t, nv),
                                     x.reshape(b, t, d), s0, w_o, ret_heads, dk, dv)
                outs.append(y.reshape(b * t, d))
                s_list.append(s_fin)
            xp, xs = outs
        fin = final_norm if i == depth - 1 else None
        xp, *w2 = _ffn(xp, ffn2_norm[i], ffn2_w_gate, ffn2_w_up, ffn2_w_down, final_g=fin, layer=i)
        xs = _ffn(xs, ffn2_norm[i], *w2, final_g=fin)

    return (xp.reshape(bp, tp, d), xs.reshape(bs, ts, d),
            jnp.stack(kp_list), jnp.stack(vp_list), jnp.stack(sp_list),
            jnp.stack(ks_list), jnp.stack(vs_list), jnp.stack(ss_list))
```

```python
import functools
import math

import jax
import jax.numpy as jnp
from jax import lax
from jax.experimental import pallas as pl
from jax.experimental.pallas import tpu as pltpu

F32 = jnp.float32
BF16 = jnp.bfloat16

NORM_EPS = 1e-6
DA_SUBLN_EPS = 1e-5
RET_GN_EPS = 1e-6
ROPE_BASE = 10000.0
CHUNK = 64
NEG_INF = -1e30

LANES = 128
V7X_VMEM_BYTES = 64 * 1024 * 1024
VMEM_LIMIT_BYTES = (V7X_VMEM_BYTES * 3) // 4

ROW_TILE = 512
ATTN_TILE = 512
RET_TILE = 256
SCORE_LOOKAHEAD = 2
WEIGHT_STAGE_CHUNKS = 8
SUM_ROWS = 16


def _params(*semantics):
    return pltpu.CompilerParams(dimension_semantics=semantics,
                                vmem_limit_bytes=VMEM_LIMIT_BYTES)


def _resident(shape):
    return pl.BlockSpec(shape, lambda *_: (0,) * len(shape),
                        pipeline_mode=pl.Buffered(1))


def _rmsnorm(x, g, eps):
    return x * lax.rsqrt(jnp.mean(x * x, axis=-1, keepdims=True) + eps) * g


def _dot(a, b):
    return jnp.dot(a, b, preferred_element_type=F32)


def _dot_nt(a, b):
    return lax.dot_general(a, b, (((1,), (1,)), ((), ())), preferred_element_type=F32)


def _dot_tn(a, b):
    return lax.dot_general(a, b, (((0,), (0,)), ((), ())), preferred_element_type=F32)


def _row_tile(n):
    return ROW_TILE if n % ROW_TILE == 0 else n


def _stage_and_cast(src_hbm, dst_vmem, stage, sem):
    rows = stage.shape[1]
    n_chunks = src_hbm.shape[0] // rows

    def chunk_copy(c, slot):
        return pltpu.make_async_copy(src_hbm.at[pl.ds(c * rows, rows)], stage.at[slot], sem.at[slot])

    chunk_copy(0, 0).start()

    def body(c, carry):
        slot = lax.rem(c, 2)

        @pl.when(c + 1 < n_chunks)
        def _():
            chunk_copy(c + 1, 1 - slot).start()

        chunk_copy(c, slot).wait()
        dst_vmem[pl.ds(pl.multiple_of(c * rows, rows), rows), :] = stage[slot].astype(BF16)
        return carry

    lax.fori_loop(0, n_chunks, body, 0)


def _stage_shape(weight_shape):
    rows, cols = weight_shape
    assert rows % WEIGHT_STAGE_CHUNKS == 0
    return (2, rows // WEIGHT_STAGE_CHUNKS, cols)


def _ffn_body(x_ref, g_ref, wg_ref, wu_ref, wd_ref, fg_ref, o_ref, f_tile, before_chunk=None):
    x = x_ref[...]
    h = _rmsnorm(x, g_ref[...], NORM_EPS).astype(BF16)
    acc = None
    for c in range(wg_ref.shape[1] // f_tile):
        sl = slice(c * f_tile, (c + 1) * f_tile)
        if before_chunk is not None:
            before_chunk(c)
        gate = _dot(h, wg_ref[:, sl])
        up = _dot(h, wu_ref[:, sl])
        a = (gate * jax.nn.sigmoid(gate) * up).astype(BF16)
        d = _dot(a, wd_ref[sl, :])
        acc = d if acc is None else acc + d
    y = x + 0.5 * acc
    if fg_ref is not None:
        y = _rmsnorm(y, fg_ref[...], NORM_EPS)
    o_ref[...] = y


def _ffn_kernel(x_ref, g_ref, wg_ref, wu_ref, wd_ref, *rest, f_tile, final):
    fg_ref, o_ref = rest if final else (None,) + rest
    _ffn_body(x_ref, g_ref, wg_ref, wu_ref, wd_ref, fg_ref, o_ref, f_tile)


def _ffn_cast_kernel(x_ref, g_ref, wg_hbm, wu_hbm, wd_hbm, *rest, f_tile, final, layer):
    fg_ref = rest[0] if final else None
    o_ref, wg_out, wu_out, wd_out, wg_v, wu_v, wd_v, stage_up, stage_down, sem_in, sem_out = (
        rest[1:] if final else rest)
    i = pl.program_id(0)
    n_chunks = wg_v.shape[1] // f_tile
    copies_out = [pltpu.make_async_copy(v, out, sem_out.at[k])
                  for k, (v, out) in enumerate(((wg_v, wg_out), (wu_v, wu_out), (wd_v, wd_out)))]

    def chunk_copies(c):
        slot, cols = c % 2, pl.ds(c * f_tile, f_tile)
        return (pltpu.make_async_copy(wg_hbm.at[layer, :, cols], stage_up.at[slot, 0], sem_in.at[slot, 0]),
                pltpu.make_async_copy(wu_hbm.at[layer, :, cols], stage_up.at[slot, 1], sem_in.at[slot, 1]),
                pltpu.make_async_copy(wd_hbm.at[layer, cols, :], stage_down.at[slot], sem_in.at[slot, 2]))

    def stage_chunk(c):
        if c + 1 < n_chunks:
            for copy in chunk_copies(c + 1):
                copy.start()
        for copy in chunk_copies(c):
            copy.wait()
        slot, sl = c % 2, slice(c * f_tile, (c + 1) * f_tile)
        wg_v[:, sl] = stage_up[slot, 0].astype(BF16)
        wu_v[:, sl] = stage_up[slot, 1].astype(BF16)
        wd_v[sl, :] = stage_down[slot].astype(BF16)

    @pl.when(i == 0)
    def _():
        for copy in chunk_copies(0):
            copy.start()
        _ffn_body(x_ref, g_ref, wg_v, wu_v, wd_v, fg_ref, o_ref, f_tile, before_chunk=stage_chunk)
        for copy_out in copies_out:
            copy_out.start()

    @pl.when(i > 0)
    def _():
        _ffn_body(x_ref, g_ref, wg_v, wu_v, wd_v, fg_ref, o_ref, f_tile)

    @pl.when(i == pl.num_programs(0) - 1)
    def _():
        for copy_out in copies_out:
            copy_out.wait()


def _ffn(x, g, wg, wu, wd, final_g=None, layer=None):
    n, d = x.shape
    f = wg.shape[-1]
    tm = _row_tile(n)
    f_tile = 256 if f % 256 == 0 else f
    final = final_g is not None
    cast = layer is not None
    row = pl.BlockSpec((tm, d), lambda i: (i, 0))
    hbm = pl.BlockSpec(memory_space=pl.ANY)
    w_specs = [hbm] * 3 if cast else [_resident((d, f)), _resident((d, f)), _resident((f, d))]
    in_specs = [row, _resident((1, d))] + w_specs
    args = [x, g.reshape(1, d), wg, wu, wd]
    if final:
        in_specs.append(_resident((1, d)))
        args.append(final_g.reshape(1, d))
    y_shape = jax.ShapeDtypeStruct((n, d), F32)
    name = "ffn_final" if final else "ffn"
    if not cast:
        return pl.pallas_call(
            functools.partial(_ffn_kernel, f_tile=f_tile, final=final),
            out_shape=y_shape, grid=(n // tm,), in_specs=in_specs, out_specs=row,
            compiler_params=_params("parallel"), name=name,
        )(*args)
    up_shape, down_shape = wg.shape[1:], wd.shape[1:]
    bf16_like = lambda shape: jax.ShapeDtypeStruct(shape, BF16)
    return pl.pallas_call(
        functools.partial(_ffn_cast_kernel, f_tile=f_tile, final=final, layer=layer),
        out_shape=(y_shape, bf16_like(up_shape), bf16_like(up_shape), bf16_like(down_shape)),
        grid=(n // tm,),
        in_specs=in_specs,
        out_specs=(row, hbm, hbm, hbm),
        scratch_shapes=[pltpu.VMEM(up_shape, BF16), pltpu.VMEM(up_shape, BF16),
                        pltpu.VMEM(down_shape, BF16), pltpu.VMEM((2, 2, d, f_tile), F32),
                        pltpu.VMEM((2, f_tile, d), F32),
                        pltpu.SemaphoreType.DMA((2, 3)), pltpu.SemaphoreType.DMA((3,))],
        compiler_params=_params("arbitrary"),
        name=name + "_cast",
    )(*args)


def _da_proj_body(x_ref, g_ref, w_ref, q_ref, k_ref, v_ref, kb_ref, vb_ref, q_scale,
                  transposed_v, n_heads):
    tm, d = x_ref.shape
    lane_tiles = d // LANES
    h = _rmsnorm(x_ref[...], g_ref[...], NORM_EPS).astype(BF16)
    q_ref[...] = (_dot(h, w_ref[:, 0:d]) * q_scale).astype(BF16)
    k = _dot(h, w_ref[:, d:2 * d])
    kb_ref[...] = k.astype(BF16)
    v = _dot(h, w_ref[:, 2 * d:3 * d])
    for j in range(lane_tiles):
        cols = slice(j * LANES, (j + 1) * LANES)
        k_ref[pl.ds(j, tm, stride=lane_tiles), :] = k[:, cols]
        head, half = divmod(j, lane_tiles // n_heads)
        v_ref[pl.ds(half * n_heads + head, tm, stride=lane_tiles), :] = v[:, cols]
    vb_ref[...] = (v.T if transposed_v else v).astype(BF16)


def _da_proj_kernel(x_ref, g_ref, w_ref, *outs, **static):
    _da_proj_body(x_ref, g_ref, w_ref, *outs, **static)


def _da_proj_cast_kernel(x_ref, g_ref, w_hbm, q_ref, k_ref, v_ref, kb_ref, vb_ref, w_out, w_v,
                         stage, sem_in, sem_out, *, layer, **static):
    i = pl.program_id(0)
    copy_out = pltpu.make_async_copy(w_v, w_out, sem_out.at[0])

    @pl.when(i == 0)
    def _():
        _stage_and_cast(w_hbm.at[layer], w_v, stage, sem_in)
        copy_out.start()

    _da_proj_body(x_ref, g_ref, w_v, q_ref, k_ref, v_ref, kb_ref, vb_ref, **static)

    @pl.when(i == pl.num_programs(0) - 1)
    def _():
        copy_out.wait()


def _da_proj(x, g, w, q_scale, transposed_v, n_heads, layer=None):
    n, d = x.shape
    tm = _row_tile(n)
    lane_tiles = d // LANES
    cast = layer is not None
    row = pl.BlockSpec((tm, d), lambda i: (i, 0))
    rows = pl.BlockSpec((tm * lane_tiles, LANES), lambda i: (i, 0))
    hbm = pl.BlockSpec(memory_space=pl.ANY)
    if transposed_v:
        vb_shape, vb_spec = (d, n), pl.BlockSpec((d, tm), lambda i: (0, i))
    else:
        vb_shape, vb_spec = (n, d), row
    out_shape = [jax.ShapeDtypeStruct((n, d), BF16),
                 jax.ShapeDtypeStruct((n * lane_tiles, LANES), F32),
                 jax.ShapeDtypeStruct((n * lane_tiles, LANES), F32),
                 jax.ShapeDtypeStruct((n, d), BF16),
                 jax.ShapeDtypeStruct(vb_shape, BF16)]
    out_specs = [row, rows, rows, row, vb_spec]
    static = dict(q_scale=q_scale, transposed_v=transposed_v, n_heads=n_heads)
    name = "da_proj_t" if transposed_v else "da_proj"
    if not cast:
        return pl.pallas_call(
            functools.partial(_da_proj_kernel, **static),
            out_shape=out_shape, grid=(n // tm,),
            in_specs=[row, _resident((1, d)), _resident((d, 3 * d))], out_specs=out_specs,
            compiler_params=_params("parallel"), name=name,
        )(x, g.reshape(1, d), w)
    w_shape = w.shape[1:]
    return pl.pallas_call(
        functools.partial(_da_proj_cast_kernel, layer=layer, **static),
        out_shape=out_shape + [jax.ShapeDtypeStruct(w_shape, BF16)], grid=(n // tm,),
        in_specs=[row, _resident((1, d)), hbm], out_specs=out_specs + [hbm],
        scratch_shapes=[pltpu.VMEM(w_shape, BF16), pltpu.VMEM(_stage_shape(w_shape), F32),
                        pltpu.SemaphoreType.DMA((2,)), pltpu.SemaphoreType.DMA((1,))],
        compiler_params=_params("arbitrary"), name=name + "_cast",
    )(x, g.reshape(1, d), w)


def _key_rows_to_cache(k_rows, b, t, n_heads, dh):
    return k_rows.reshape(b, t, n_heads, 2, dh)


def _value_rows_to_cache(v_rows, b, t, n_heads, dh):
    v = v_rows.reshape(b, t, 2, n_heads, dh)
    return jnp.swapaxes(v, 2, 3).reshape(b, t, n_heads, 2 * dh)


def _key_cache_to_rows(k_cache):
    b, t, n_heads, _, dh = k_cache.shape
    return k_cache.reshape(b, t * n_heads * 2, dh)


def _value_cache_to_rows(v_cache):
    b, t, n_heads, dv = v_cache.shape
    v = v_cache.reshape(b, t, n_heads, 2, dv // 2)
    return jnp.swapaxes(v, 2, 3).reshape(b, t * 2 * n_heads, dv // 2)


def _da_lambda(lq1_ref, lk1_ref, lq2_ref, lk2_ref, lam_init):
    s1 = jnp.sum(lq1_ref[...] * lk1_ref[...], axis=1, keepdims=True)
    s2 = jnp.sum(lq2_ref[...] * lk2_ref[...], axis=1, keepdims=True)
    return jnp.exp(s1) - jnp.exp(s2) + lam_init


def _da_head_out(o, sub_ref, lam_init):
    return (_rmsnorm(o, sub_ref[...], DA_SUBLN_EPS) * (1.0 - lam_init)).astype(BF16)


def _da_attn_kernel(lq1_ref, lk1_ref, lq2_ref, lk2_ref, q_ref, k_ref, vt_ref, x_ref, wo_ref,
                    sub_ref, o_ref, acc_ref, cat_ref, *, n_heads, dh, tq, lam_init):
    i = pl.program_id(1)
    dv = 2 * dh
    lam = _da_lambda(lq1_ref, lk1_ref, lq2_ref, lk2_ref, lam_init)
    key_chunk = lax.broadcasted_iota(jnp.int32, (tq, tq), 0) // CHUNK
    qry_chunk = lax.broadcasted_iota(jnp.int32, (tq, tq), 1) // CHUNK
    diag_mask = key_chunk <= qry_chunk

    acc_ref[...] = jnp.zeros_like(acc_ref)

    def absorb(j, stats, masked):
        ks = pl.multiple_of(j * tq, tq)
        new = []
        n_chains = 2 * n_heads

        def scores(chain):
            cols = slice(chain * dh, (chain + 1) * dh)
            return _dot_nt(k_ref[0, pl.ds(ks, tq), cols], q_ref[0, :, cols])

        pending = [scores(c) for c in range(SCORE_LOOKAHEAD)]
        ones = jnp.ones((SUM_ROWS, tq), BF16)
        for chain in range(n_chains):
            h = chain // 2
            s = pending.pop(0)
            if chain + SCORE_LOOKAHEAD < n_chains:
                pending.append(scores(chain + SCORE_LOOKAHEAD))
            if masked:
                s = jnp.where(diag_mask, s, NEG_INF)
            m_new = jnp.maximum(stats[chain], jnp.max(s, axis=0, keepdims=True))
            alpha = jnp.exp2(stats[chain] - m_new)
            p = jnp.exp2(s - m_new).astype(BF16)
            vt = jnp.concatenate([vt_ref[h * dv:(h + 1) * dv, pl.ds(ks, tq)], ones], axis=0)
            acc_ref[chain] = alpha * acc_ref[chain] + _dot(vt, p)
            new.append(m_new)
        return tuple(new)

    m0 = jnp.full((1, tq), NEG_INF, F32)
    stats = lax.fori_loop(0, i, functools.partial(absorb, masked=False), (m0,) * (2 * n_heads))
    absorb(i, stats, masked=True)

    for h in range(n_heads):
        l1, l2 = acc_ref[2 * h, dv:dv + 1, :], acc_ref[2 * h + 1, dv:dv + 1, :]
        o = acc_ref[2 * h, :dv, :] / l1 - lam * (acc_ref[2 * h + 1, :dv, :] / l2)
        on = o * lax.rsqrt(jnp.mean(o * o, axis=0, keepdims=True) + DA_SUBLN_EPS) * sub_ref[...]
        cat_ref[h * dv:(h + 1) * dv, :] = (on * (1.0 - lam_init)).astype(BF16)

    o_ref[0] = x_ref[0] + _dot_tn(cat_ref[...], wo_ref[...])


def _da_attn_prompt(lams, q, kb, vt, x, wo, subln, n_heads, lam_init):
    b, t, d = x.shape
    dh = d // (2 * n_heads)
    tq = ATTN_TILE if t % ATTN_TILE == 0 else t
    assert tq % CHUNK == 0
    blk = pl.BlockSpec((1, tq, d), lambda bi, i: (bi, i, 0))
    seq = pl.BlockSpec((1, t, d), lambda bi, i: (bi, 0, 0), pipeline_mode=pl.Buffered(1))
    seq_t = pl.BlockSpec((d, t), lambda bi, i: (0, bi), pipeline_mode=pl.Buffered(1))
    lam_spec = _resident((1, dh))
    return pl.pallas_call(
        functools.partial(_da_attn_kernel, n_heads=n_heads, dh=dh, tq=tq, lam_init=lam_init),
        out_shape=jax.ShapeDtypeStruct((b, t, d), F32),
        grid=(b, t // tq),
        in_specs=[lam_spec] * 4 + [blk, seq, seq_t, blk, _resident((d, d)),
                                   _resident((2 * dh, 1))],
        out_specs=blk,
        scratch_shapes=[pltpu.VMEM((2 * n_heads, 2 * dh + SUM_ROWS, tq), F32),
                        pltpu.VMEM((d, tq), BF16)],
        compiler_params=_params("parallel", "arbitrary"),
        name="da_attn_prompt",
    )(*lams, q, kb, vt, x, wo, subln.reshape(2 * dh, 1))


def _da_attn_sample_kernel(lq1_ref, lk1_ref, lq2_ref, lk2_ref, q_ref, kc_ref, vc_ref, kn_ref,
                           vn_ref, x_ref, wo_ref, sub_ref, o_ref, cat_ref, *, n_heads, dh,
                           lam_init):
    lam = _da_lambda(lq1_ref, lk1_ref, lq2_ref, lk2_ref, lam_init)
    rows_per_token = 2 * n_heads
    past = kc_ref.shape[1] // rows_per_token

    def cached(ref, row):
        return ref.at[0][pl.ds(row, past, stride=rows_per_token), :].astype(BF16)

    for h in range(n_heads):
        c0 = h * 2 * dh
        probs = []
        for c in range(2):
            qh = q_ref[0, :, c0 + c * dh:c0 + (c + 1) * dh]
            sc = _dot_nt(qh, cached(kc_ref, 2 * h + c))
            sn = _dot_nt(qh, kn_ref[0, :, c0 + c * dh:c0 + (c + 1) * dh])
            m = jnp.maximum(jnp.max(sc, axis=1, keepdims=True), jnp.max(sn, axis=1, keepdims=True))
            pc = jnp.exp2(sc - m)
            pn = jnp.exp2(sn - m)
            l = jnp.sum(pc, axis=1, keepdims=True) + jnp.sum(pn, axis=1, keepdims=True)
            probs.append((pc / l, pn / l))
        ac = (probs[0][0] - lam * probs[1][0]).astype(BF16)
        an = (probs[0][1] - lam * probs[1][1]).astype(BF16)
        vc = jnp.concatenate([cached(vc_ref, h), cached(vc_ref, n_heads + h)], axis=1)
        o = _dot(ac, vc) + _dot(an, vn_ref[0, :, c0:c0 + 2 * dh])
        cat_ref[:, c0:c0 + 2 * dh] = _da_head_out(o, sub_ref, lam_init)
    o_ref[0] = x_ref[0] + _dot(cat_ref[...], wo_ref[...])


def _da_attn_sample(lams, q, k_rows, v_rows, kb, vb, x, wo, subln, n_heads, lam_init):
    b, ts, d = x.shape
    dh = d // (2 * n_heads)
    past = k_rows.shape[1] // (2 * n_heads)
    assert past % CHUNK == 0 and ts <= CHUNK
    new = pl.BlockSpec((1, ts, d), lambda bi: (bi, 0, 0))
    cache = pl.BlockSpec((1, past * 2 * n_heads, dh), lambda bi: (bi, 0, 0))
    lam_spec = _resident((1, dh))
    return pl.pallas_call(
        functools.partial(_da_attn_sample_kernel, n_heads=n_heads, dh=dh, lam_init=lam_init),
        out_shape=jax.ShapeDtypeStruct((b, ts, d), F32),
        grid=(b,),
        in_specs=[lam_spec] * 4 + [new, cache, cache, new, new, new, _resident((d, d)),
                                   _resident((1, 2 * dh))],
        out_specs=new,
        scratch_shapes=[pltpu.VMEM((ts, d), BF16)],
        compiler_params=_params("parallel"),
        name="da_attn_sample",
    )(*lams, q, k_rows, v_rows, kb, vb, x, wo, subln.reshape(1, 2 * dh))


def _rope_tables(pos, dk):
    angle = 1.0 / (ROPE_BASE ** jnp.linspace(0.0, 1.0, dk // 2, dtype=F32))
    angle = jnp.repeat(angle, 2)
    theta = pos[:, None] * angle[None, :]
    sign = jnp.where(jnp.arange(dk) % 2 == 0, -1.0, 1.0).astype(F32)
    return jnp.cos(theta), jnp.sin(theta) * sign[None, :]


def _ret_proj_kernel(x_ref, g_ref, w_ref, cos_ref, sin_ref, q_ref, k_ref, v_ref, gate_ref, *,
                     n_heads, dk, k_scale):
    nq = n_heads * dk
    nv = v_ref.shape[1]
    h = _rmsnorm(x_ref[...], g_ref[...], NORM_EPS).astype(BF16)
    cos = cos_ref[...]
    sin = sin_ref[...]
    even = lax.broadcasted_iota(jnp.int32, cos.shape, 1) % 2 == 0

    def rotary(a):
        rot = jnp.where(even, pltpu.roll(a, dk - 1, 1), pltpu.roll(a, 1, 1))
        return a * cos + rot * sin

    q = _dot(h, w_ref[:, 0:nq])
    k = _dot(h, w_ref[:, nq:2 * nq])
    for hd in range(n_heads):
        sl = slice(hd * dk, (hd + 1) * dk)
        q_ref[:, sl] = rotary(q[:, sl]).astype(BF16)
        k_ref[:, sl] = (rotary(k[:, sl]) * k_scale).astype(BF16)
    v_ref[...] = _dot(h, w_ref[:, 2 * nq:2 * nq + nv]).astype(BF16)
    gate_ref[...] = _dot(h, w_ref[:, 2 * nq + nv:2 * nq + 2 * nv])


def _ret_proj(x, g, w, cos, sin, n_heads, dk, dv):
    n, d = x.shape
    nq, nv = n_heads * dk, n_heads * dv
    tm = _row_tile(cos.shape[0])
    pos_blocks = cos.shape[0] // tm
    row = lambda width: pl.BlockSpec((tm, width), lambda i: (i, 0))
    tab = pl.BlockSpec((tm, dk), lambda i: (i % pos_blocks, 0))
    return pl.pallas_call(
        functools.partial(_ret_proj_kernel, n_heads=n_heads, dk=dk, k_scale=dk ** -0.5),
        out_shape=(jax.ShapeDtypeStruct((n, nq), BF16), jax.ShapeDtypeStruct((n, nq), BF16),
                   jax.ShapeDtypeStruct((n, nv), BF16), jax.ShapeDtypeStruct((n, nv), F32)),
        grid=(n // tm,),
        in_specs=[row(d), _resident((1, d)), _resident((d, 2 * nq + 2 * nv)), tab, tab],
        out_specs=(row(nq), row(nq), row(nv), row(nv)),
        compiler_params=_params("parallel"),
        name="ret_proj",
    )(x, g.reshape(1, d), w, cos, sin)


def _ret_core_kernel(lg_ref, q_ref, k_ref, v_ref, gate_ref, x_ref, s0_ref, wo_ref, o_ref,
                     sfin_ref, s_scr, y_scr, decay_scr, *, n_heads, dk, dv, blk):
    c = pl.program_id(1)

    row = lax.broadcasted_iota(jnp.int32, (blk, blk), 0)
    col = lax.broadcasted_iota(jnp.int32, (blk, blk), 1)
    rel = (row - col).astype(F32)
    idx = lax.broadcasted_iota(jnp.int32, (blk, 1), 0).astype(F32)

    @pl.when(c == 0)
    def _():
        s_scr[...] = s0_ref[0]
        for h in range(n_heads):
            decay_scr[h] = jnp.where(rel >= 0, jnp.exp(jnp.maximum(rel, 0.0) * lg_ref[h]), 0.0)

    first = []
    for h in range(n_heads):
        lg = lg_ref[h]
        qh = q_ref[0, :, h * dk:(h + 1) * dk]
        kh = k_ref[0, :, h * dk:(h + 1) * dk]
        vh = v_ref[0, :, h * dv:(h + 1) * dv]
        s = s_scr[h]
        qk = _dot_nt(qh, kh)
        cross = _dot(qh, s.astype(BF16))
        kd = (kh.astype(F32) * jnp.exp((blk - 1.0 - idx) * lg)).astype(BF16)
        s_decay = jnp.exp(jnp.full((1, dv), float(blk), F32) * lg)
        s_scr[h] = s * s_decay + _dot_tn(kd, vh)
        first.append((qk, cross))

    for h, (qk, cross) in enumerate(first):
        vh = v_ref[0, :, h * dv:(h + 1) * dv]
        inner = _dot((qk * decay_scr[h]).astype(BF16), vh)
        o = inner + cross * jnp.exp((idx + 1.0) * lg_ref[h])
        on = o * lax.rsqrt(jnp.mean(o * o, axis=-1, keepdims=True) + RET_GN_EPS)
        gate = gate_ref[0, :, h * dv:(h + 1) * dv]
        y_scr[:, h * dv:(h + 1) * dv] = (gate * jax.nn.sigmoid(gate) * on).astype(BF16)

    o_ref[0] = x_ref[0] + _dot(y_scr[...], wo_ref[...])

    @pl.when(c == pl.num_programs(1) - 1)
    def _():
        sfin_ref[0] = s_scr[...]


def _ret_core(lg, q, k, v, gate, x, s0, wo, n_heads, dk, dv):
    b, t, d = x.shape
    blk = RET_TILE if t % RET_TILE == 0 else t
    nq, nv = n_heads * dk, n_heads * dv
    tok = lambda width: pl.BlockSpec((1, blk, width), lambda bi, c: (bi, c, 0))
    state = pl.BlockSpec((1, n_heads, dk, dv), lambda bi, c: (bi, 0, 0, 0))
    return pl.pallas_call(
        functools.partial(_ret_core_kernel, n_heads=n_heads, dk=dk, dv=dv, blk=blk),
        out_shape=(jax.ShapeDtypeStruct((b, t, d), F32),
                   jax.ShapeDtypeStruct((b, n_heads, dk, dv), F32)),
        grid=(b, t // blk),
        in_specs=[pl.BlockSpec(memory_space=pltpu.SMEM), tok(nq), tok(nq), tok(nv), tok(nv),
                  tok(d), state, _resident((nv, d))],
        out_specs=(tok(d), state),
        scratch_shapes=[pltpu.VMEM((n_heads, dk, dv), F32), pltpu.VMEM((blk, nv), BF16),
                        pltpu.VMEM((n_heads, blk, blk), F32)],
        compiler_params=_params("parallel", "arbitrary"),
        name="ret_core",
    )(lg, q, k, v, gate, x, s0, wo)


def _lambda_init(layer_idx):
    return 0.8 - 0.6 * math.exp(-0.3 * layer_idx)


def kernel(x_prompt, x_sample, cache_diff_k, cache_diff_v, state_ret, ffn1_norm, ffn1_w_gate, ffn1_w_up, ffn1_w_down, mix_norm, da_w_qkv, da_lambda_q1, da_lambda_k1, da_lambda_q2, da_lambda_k2, da_subln, da_w_o, ret_w_in, ret_w_o, ffn2_norm, ffn2_w_gate, ffn2_w_up, ffn2_w_down, final_norm):
    bp, tp, d = x_prompt.shape
    bs, ts, _ = x_sample.shape
    past = cache_diff_k.shape[2]
    da_heads = cache_diff_k.shape[3]
    dh = cache_diff_k.shape[5]
    ret_heads, dk, dv = state_ret.shape[2:]
    depth = ffn1_norm.shape[0]
    bf = lambda w: w.astype(BF16)

    xp = x_prompt.reshape(bp * tp, d)
    xs = x_sample.reshape(bs * ts, d)
    kp_list, vp_list, sp_list = [], [], []
    ks_list, vs_list, ss_list = [], [], []
    for i in range(depth):
        xp, *w1 = _ffn(xp, ffn1_norm[i], ffn1_w_gate, ffn1_w_up, ffn1_w_down, layer=i)
        xs = _ffn(xs, ffn1_norm[i], *w1)
        if i % 2 == 0:
            a = i // 2
            lam_init = _lambda_init(i)
            w_o = bf(da_w_o[a])
            lams = [v[a].reshape(1, dh) for v in (da_lambda_q1, da_lambda_k1, da_lambda_q2, da_lambda_k2)]
            q_scale = dh ** -0.5 * math.log2(math.e)
            qp, kp, vp, kpb, vpt, w_qkv = _da_proj(xp, mix_norm[i], da_w_qkv, q_scale, True,
                                                   da_heads, layer=a)
            qs, ks, vs, ksb, vsb = _da_proj(xs, mix_norm[i], w_qkv, q_scale, False, da_heads)
            seq = lambda z, b, t: z.reshape(b, t, d)
            xp = _da_attn_prompt(lams, seq(qp, bp, tp), seq(kpb, bp, tp), vpt,
                                 seq(xp, bp, tp), w_o, da_subln[a], da_heads,
                                 lam_init).reshape(bp * tp, d)
            xs = _da_attn_sample(lams, seq(qs, bs, ts), _key_cache_to_rows(cache_diff_k[a]),
                                 _value_cache_to_rows(cache_diff_v[a]), seq(ksb, bs, ts),
                                 seq(vsb, bs, ts), seq(xs, bs, ts), w_o, da_subln[a], da_heads,
                                 lam_init).reshape(bs * ts, d)
            kp_list.append(_key_rows_to_cache(kp, bp, tp, da_heads, dh))
            vp_list.append(_value_rows_to_cache(vp, bp, tp, da_heads, dh))
            ks_list.append(_key_rows_to_cache(ks, bs, ts, da_heads, dh))
            vs_list.append(_value_rows_to_cache(vs, bs, ts, da_heads, dh))
        else:
            r = i // 2
            w_in, w_o = bf(ret_w_in[r]), bf(ret_w_o[r])
            lg = jnp.log1p(-jnp.exp2(-5.0 - jnp.arange(ret_heads, dtype=F32)))
            nq, nv = ret_heads * dk, ret_heads * dv
            groups = ((xp, bp, tp, jnp.arange(tp, dtype=F32),
                       jnp.zeros((bp, ret_heads, dk, dv), F32), sp_list),
                      (xs, bs, ts, past + jnp.arange(ts, dtype=F32),
                       state_ret[r].astype(F32), ss_list))
            outs = []
            for x, b, t, pos, s0, s_list in groups:
                cos, sin = _rope_tables(pos, dk)
                if t % ROW_TILE != 0:
                    cos, sin = jnp.tile(cos, (b, 1)), jnp.tile(sin, (b, 1))
                q, k, v, gate = _ret_proj(x, mix_norm[i], w_in, cos, sin, ret_heads, dk, dv)
                y, s_fin = _ret_core(lg, q.reshape(b, t, nq), k.reshape(b, t, nq),
                                     v.reshape(b, t, nv), gate.reshape(b, t, nv),
                                     x.reshape(b, t, d), s0, w_o, ret_heads, dk, dv)
                outs.append(y.reshape(b * t, d))
                s_list.append(s_fin)
            xp, xs = outs
        fin = final_norm if i == depth - 1 else None
        xp, *w2 = _ffn(xp, ffn2_norm[i], ffn2_w_gate, ffn2_w_up, ffn2_w_down, final_g=fin, layer=i)
        xs = _ffn(xs, ffn2_norm[i], *w2, final_g=fin)

    return (xp.reshape(bp, tp, d), xs.reshape(bs, ts, d),
            jnp.stack(kp_list), jnp.stack(vp_list), jnp.stack(sp_list),
            jnp.stack(ks_list), jnp.stack(vs_list), jnp.stack(ss_list))
```

```python
import functools
import math

import jax
import jax.numpy as jnp
from jax import lax
from jax.experimental import pallas as pl
from jax.experimental.pallas import tpu as pltpu

F32 = jnp.float32
BF16 = jnp.bfloat16

NORM_EPS = 1e-6
DA_SUBLN_EPS = 1e-5
RET_GN_EPS = 1e-6
ROPE_BASE = 10000.0
CHUNK = 64
NEG_INF = -1e30

LANES = 128
V7X_VMEM_BYTES = 64 * 1024 * 1024
VMEM_LIMIT_BYTES = (V7X_VMEM_BYTES * 3) // 4

ROW_TILE = 512
ATTN_TILE = 512
RET_TILE = 256
SCORE_LOOKAHEAD = 2
WEIGHT_STAGE_CHUNKS = 8
SUM_ROWS = 16


def _params(*semantics):
    return pltpu.CompilerParams(dimension_semantics=semantics,
                                vmem_limit_bytes=VMEM_LIMIT_BYTES)


def _resident(shape):
    return pl.BlockSpec(shape, lambda *_: (0,) * len(shape),
                        pipeline_mode=pl.Buffered(1))


def _rmsnorm(x, g, eps):
    return x * lax.rsqrt(jnp.mean(x * x, axis=-1, keepdims=True) + eps) * g


def _dot(a, b):
    return jnp.dot(a, b, preferred_element_type=F32)


def _dot_nt(a, b):
    return lax.dot_general(a, b, (((1,), (1,)), ((), ())), preferred_element_type=F32)


def _dot_tn(a, b):
    return lax.dot_general(a, b, (((0,), (0,)), ((), ())), preferred_element_type=F32)


def _row_tile(n):
    return ROW_TILE if n % ROW_TILE == 0 else n


def _stage_and_cast(src_hbm, dst_vmem, stage, sem):
    rows = stage.shape[1]
    n_chunks = src_hbm.shape[0] // rows

    def chunk_copy(c, slot):
        return pltpu.make_async_copy(src_hbm.at[pl.ds(c * rows, rows)], stage.at[slot], sem.at[slot])

    chunk_copy(0, 0).start()

    def body(c, carry):
        slot = lax.rem(c, 2)

        @pl.when(c + 1 < n_chunks)
        def _():
            chunk_copy(c + 1, 1 - slot).start()

        chunk_copy(c, slot).wait()
        dst_vmem[pl.ds(pl.multiple_of(c * rows, rows), rows), :] = stage[slot].astype(BF16)
        return carry

    lax.fori_loop(0, n_chunks, body, 0)


def _stage_shape(weight_shape):
    rows, cols = weight_shape
    assert rows % WEIGHT_STAGE_CHUNKS == 0
    return (2, rows // WEIGHT_STAGE_CHUNKS, cols)


def _ffn_body(x_ref, g_ref, wg_ref, wu_ref, wd_ref, fg_ref, o_ref, f_tile, before_chunk=None):
    x = x_ref[...]
    h = _rmsnorm(x, g_ref[...], NORM_EPS).astype(BF16)
    acc = None
    for c in range(wg_ref.shape[1] // f_tile):
        sl = slice(c * f_tile, (c + 1) * f_tile)
        if before_chunk is not None:
            before_chunk(c)
        gate = _dot(h, wg_ref[:, sl])
        up = _dot(h, wu_ref[:, sl])
        a = (gate * jax.nn.sigmoid(gate) * up).astype(BF16)
        d = _dot(a, wd_ref[sl, :])
        acc = d if acc is None else acc + d
    y = x + 0.5 * acc
    if fg_ref is not None:
        y = _rmsnorm(y, fg_ref[...], NORM_EPS)
    o_ref[...] = y


def _ffn_kernel(x_ref, g_ref, wg_ref, wu_ref, wd_ref, *rest, f_tile, final):
    fg_ref, o_ref = rest if final else (None,) + rest
    _ffn_body(x_ref, g_ref, wg_ref, wu_ref, wd_ref, fg_ref, o_ref, f_tile)


def _ffn_cast_kernel(x_ref, g_ref, wg_hbm, wu_hbm, wd_hbm, *rest, f_tile, final, layer):
    fg_ref = rest[0] if final else None
    o_ref, wg_out, wu_out, wd_out, wg_v, wu_v, wd_v, stage_up, stage_down, sem_in, sem_out = (
        rest[1:] if final else rest)
    i = pl.program_id(0)
    n_chunks = wg_v.shape[1] // f_tile
    copies_out = [pltpu.make_async_copy(v, out, sem_out.at[k])
                  for k, (v, out) in enumerate(((wg_v, wg_out), (wu_v, wu_out), (wd_v, wd_out)))]

    def chunk_copies(c):
        slot, cols = c % 2, pl.ds(c * f_tile, f_tile)
        return (pltpu.make_async_copy(wg_hbm.at[layer, :, cols], stage_up.at[slot, 0], sem_in.at[slot, 0]),
                pltpu.make_async_copy(wu_hbm.at[layer, :, cols], stage_up.at[slot, 1], sem_in.at[slot, 1]),
                pltpu.make_async_copy(wd_hbm.at[layer, cols, :], stage_down.at[slot], sem_in.at[slot, 2]))

    def stage_chunk(c):
        if c + 1 < n_chunks:
            for copy in chunk_copies(c + 1):
                copy.start()
        for copy in chunk_copies(c):
            copy.wait()
        slot, sl = c % 2, slice(c * f_tile, (c + 1) * f_tile)
        wg_v[:, sl] = stage_up[slot, 0].astype(BF16)
        wu_v[:, sl] = stage_up[slot, 1].astype(BF16)
        wd_v[sl, :] = stage_down[slot].astype(BF16)

    @pl.when(i == 0)
    def _():
        for copy in chunk_copies(0):
            copy.start()
        _ffn_body(x_ref, g_ref, wg_v, wu_v, wd_v, fg_ref, o_ref, f_tile, before_chunk=stage_chunk)
        for copy_out in copies_out:
            copy_out.start()

    @pl.when(i > 0)
    def _():
        _ffn_body(x_ref, g_ref, wg_v, wu_v, wd_v, fg_ref, o_ref, f_tile)

    @pl.when(i == pl.num_programs(0) - 1)
    def _():
        for copy_out in copies_out:
            copy_out.wait()


def _ffn(x, g, wg, wu, wd, final_g=None, layer=None):
    n, d = x.shape
    f = wg.shape[-1]
    tm = _row_tile(n)
    f_tile = 256 if f % 256 == 0 else f
    final = final_g is not None
    cast = layer is not None
    row = pl.BlockSpec((tm, d), lambda i: (i, 0))
    hbm = pl.BlockSpec(memory_space=pl.ANY)
    w_specs = [hbm] * 3 if cast else [_resident((d, f)), _resident((d, f)), _resident((f, d))]
    in_specs = [row, _resident((1, d))] + w_specs
    args = [x, g.reshape(1, d), wg, wu, wd]
    if final:
        in_specs.append(_resident((1, d)))
        args.append(final_g.reshape(1, d))
    y_shape = jax.ShapeDtypeStruct((n, d), F32)
    name = "ffn_final" if final else "ffn"
    if not cast:
        return pl.pallas_call(
            functools.partial(_ffn_kernel, f_tile=f_tile, final=final),
            out_shape=y_shape, grid=(n // tm,), in_specs=in_specs, out_specs=row,
            compiler_params=_params("parallel"), name=name,
        )(*args)
    up_shape, down_shape = wg.shape[1:], wd.shape[1:]
    bf16_like = lambda shape: jax.ShapeDtypeStruct(shape, BF16)
    return pl.pallas_call(
        functools.partial(_ffn_cast_kernel, f_tile=f_tile, final=final, layer=layer),
        out_shape=(y_shape, bf16_like(up_shape), bf16_like(up_shape), bf16_like(down_shape)),
        grid=(n // tm,),
        in_specs=in_specs,
        out_specs=(row, hbm, hbm, hbm),
        scratch_shapes=[pltpu.VMEM(up_shape, BF16), pltpu.VMEM(up_shape, BF16),
                        pltpu.VMEM(down_shape, BF16), pltpu.VMEM((2, 2, d, f_tile), F32),
                        pltpu.VMEM((2, f_tile, d), F32),
                        pltpu.SemaphoreType.DMA((2, 3)), pltpu.SemaphoreType.DMA((3,))],
        compiler_params=_params("arbitrary"),
        name=name + "_cast",
    )(*args)


def _da_proj_body(x_ref, g_ref, w_ref, q_ref, k_ref, v_ref, kb_ref, vb_ref, q_scale,
                  transposed_v, n_heads):
    tm, d = x_ref.shape
    lane_tiles = d // LANES
    h = _rmsnorm(x_ref[...], g_ref[...], NORM_EPS).astype(BF16)
    q_ref[...] = (_dot(h, w_ref[:, 0:d]) * q_scale).astype(BF16)
    k = _dot(h, w_ref[:, d:2 * d])
    kb_ref[...] = k.astype(BF16)
    v = _dot(h, w_ref[:, 2 * d:3 * d])
    for j in range(lane_tiles):
        cols = slice(j * LANES, (j + 1) * LANES)
        k_ref[pl.ds(j, tm, stride=lane_tiles), :] = k[:, cols]
        head, half = divmod(j, lane_tiles // n_heads)
        v_ref[pl.ds(half * n_heads + head, tm, stride=lane_tiles), :] = v[:, cols]
    vb_ref[...] = (v.T if transposed_v else v).astype(BF16)


def _da_proj_kernel(x_ref, g_ref, w_ref, *outs, **static):
    _da_proj_body(x_ref, g_ref, w_ref, *outs, **static)


def _da_proj_cast_kernel(x_ref, g_ref, w_hbm, q_ref, k_ref, v_ref, kb_ref, vb_ref, w_out, w_v,
                         stage, sem_in, sem_out, *, layer, **static):
    i = pl.program_id(0)
    copy_out = pltpu.make_async_copy(w_v, w_out, sem_out.at[0])

    @pl.when(i == 0)
    def _():
        _stage_and_cast(w_hbm.at[layer], w_v, stage, sem_in)
        copy_out.start()

    _da_proj_body(x_ref, g_ref, w_v, q_ref, k_ref, v_ref, kb_ref, vb_ref, **static)

    @pl.when(i == pl.num_programs(0) - 1)
    def _():
        copy_out.wait()


def _da_proj(x, g, w, q_scale, transposed_v, n_heads, layer=None):
    n, d = x.shape
    tm = _row_tile(n)
    lane_tiles = d // LANES
    cast = layer is not None
    row = pl.BlockSpec((tm, d), lambda i: (i, 0))
    rows = pl.BlockSpec((tm * lane_tiles, LANES), lambda i: (i, 0))
    hbm = pl.BlockSpec(memory_space=pl.ANY)
    if transposed_v:
        vb_shape, vb_spec = (d, n), pl.BlockSpec((d, tm), lambda i: (0, i))
    else:
        vb_shape, vb_spec = (n, d), row
    out_shape = [jax.ShapeDtypeStruct((n, d), BF16),
                 jax.ShapeDtypeStruct((n * lane_tiles, LANES), F32),
                 jax.ShapeDtypeStruct((n * lane_tiles, LANES), F32),
                 jax.ShapeDtypeStruct((n, d), BF16),
                 jax.ShapeDtypeStruct(vb_shape, BF16)]
    out_specs = [row, rows, rows, row, vb_spec]
    static = dict(q_scale=q_scale, transposed_v=transposed_v, n_heads=n_heads)
    name = "da_proj_t" if transposed_v else "da_proj"
    if not cast:
        return pl.pallas_call(
            functools.partial(_da_proj_kernel, **static),
            out_shape=out_shape, grid=(n // tm,),
            in_specs=[row, _resident((1, d)), _resident((d, 3 * d))], out_specs=out_specs,
            compiler_params=_params("parallel"), name=name,
        )(x, g.reshape(1, d), w)
    w_shape = w.shape[1:]
    return pl.pallas_call(
        functools.partial(_da_proj_cast_kernel, layer=layer, **static),
        out_shape=out_shape + [jax.ShapeDtypeStruct(w_shape, BF16)], grid=(n // tm,),
        in_specs=[row, _resident((1, d)), hbm], out_specs=out_specs + [hbm],
        scratch_shapes=[pltpu.VMEM(w_shape, BF16), pltpu.VMEM(_stage_shape(w_shape), F32),
                        pltpu.SemaphoreType.DMA((2,)), pltpu.SemaphoreType.DMA((1,))],
        compiler_params=_params("arbitrary"), name=name + "_cast",
    )(x, g.reshape(1, d), w)


def _key_rows_to_cache(k_rows, b, t, n_heads, dh):
    return k_rows.reshape(b, t, n_heads, 2, dh)


def _value_rows_to_cache(v_rows, b, t, n_heads, dh):
    v = v_rows.reshape(b, t, 2, n_heads, dh)
    return jnp.swapaxes(v, 2, 3).reshape(b, t, n_heads, 2 * dh)


def _key_cache_to_rows(k_cache):
    b, t, n_heads, _, dh = k_cache.shape
    return k_cache.reshape(b, t * n_heads * 2, dh)


def _value_cache_to_rows(v_cache):
    b, t, n_heads, dv = v_cache.shape
    v = v_cache.reshape(b, t, n_heads, 2, dv // 2)
    return jnp.swapaxes(v, 2, 3).reshape(b, t * 2 * n_heads, dv // 2)


def _da_lambda(lq1_ref, lk1_ref, lq2_ref, lk2_ref, lam_init):
    s1 = jnp.sum(lq1_ref[...] * lk1_ref[...], axis=1, keepdims=True)
    s2 = jnp.sum(lq2_ref[...] * lk2_ref[...], axis=1, keepdims=True)
    return jnp.exp(s1) - jnp.exp(s2) + lam_init


def _da_head_out(o, sub_ref, lam_init):
    return (_rmsnorm(o, sub_ref[...], DA_SUBLN_EPS) * (1.0 - lam_init)).astype(BF16)


def _da_attn_kernel(lq1_ref, lk1_ref, lq2_ref, lk2_ref, q_ref, k_ref, vt_ref, x_ref, wo_ref,
                    sub_ref, o_ref, acc_ref, cat_ref, *, n_heads, dh, tq, lam_init):
    i = pl.program_id(1)
    dv = 2 * dh
    qn = tq // 2
    n_chains = 2 * n_heads
    lam = _da_lambda(lq1_ref, lk1_ref, lq2_ref, lk2_ref, lam_init)
    key_chunk = lax.broadcasted_iota(jnp.int32, (qn, qn), 0) // CHUNK
    qry_chunk = lax.broadcasted_iota(jnp.int32, (qn, qn), 1) // CHUNK
    diag_mask = key_chunk <= qry_chunk
    ones = jnp.ones((SUM_ROWS, tq), BF16)

    acc_ref[...] = jnp.zeros_like(acc_ref)

    def run_pieces(pieces, stats):
        stats = dict(stats)

        def scores(piece):
            chain, q0, qw, k0, kn, _ = piece
            cols = slice(chain * dh, (chain + 1) * dh)
            return _dot_nt(k_ref[0, pl.ds(k0, kn), cols], q_ref[0, q0:q0 + qw, cols])

        pending = [scores(p) for p in pieces[:SCORE_LOOKAHEAD]]
        for n, (chain, q0, qw, k0, kn, masked) in enumerate(pieces):
            s = pending.pop(0)
            if n + SCORE_LOOKAHEAD < len(pieces):
                pending.append(scores(pieces[n + SCORE_LOOKAHEAD]))
            if masked == qw:
                s = jnp.where(diag_mask, s, NEG_INF)
            elif masked:
                s = jnp.concatenate([jnp.where(diag_mask, s[:, :masked], NEG_INF), s[:, masked:]],
                                    axis=1)
            m_old = stats[chain, q0]
            m_new = jnp.maximum(m_old, jnp.max(s, axis=0, keepdims=True))
            alpha = jnp.exp2(m_old - m_new)
            p = jnp.exp2(s - m_new).astype(BF16)
            h = chain // 2
            vt = jnp.concatenate([vt_ref[h * dv:(h + 1) * dv, pl.ds(k0, kn)], ones[:, :kn]], axis=0)
            lanes = slice(q0, q0 + qw)
            acc_ref[chain, :, lanes] = alpha * acc_ref[chain, :, lanes] + _dot(vt, p)
            stats[chain, q0] = m_new
        return stats

    def full_block(j, maxima):
        k0 = pl.multiple_of(j * tq, tq)
        stats = run_pieces([(c, 0, tq, k0, tq, 0) for c in range(n_chains)],
                           {(c, 0): maxima[c] for c in range(n_chains)})
        return tuple(stats[c, 0] for c in range(n_chains))

    m0 = jnp.full((1, tq), NEG_INF, F32)
    maxima = lax.fori_loop(0, i, full_block, (m0,) * n_chains)

    k_diag = pl.multiple_of(i * tq, tq)
    stats = run_pieces([(c, 0, tq, k_diag, qn, qn) for c in range(n_chains)],
                       {(c, 0): maxima[c] for c in range(n_chains)})
    run_pieces([(c, qn, qn, k_diag + qn, qn, qn) for c in range(n_chains)],
               {(c, qn): stats[c, 0][:, qn:] for c in range(n_chains)})

    for h in range(n_heads):
        l1, l2 = acc_ref[2 * h, dv:dv + 1, :], acc_ref[2 * h + 1, dv:dv + 1, :]
        o = acc_ref[2 * h, :dv, :] / l1 - lam * (acc_ref[2 * h + 1, :dv, :] / l2)
        on = o * lax.rsqrt(jnp.mean(o * o, axis=0, keepdims=True) + DA_SUBLN_EPS) * sub_ref[...]
        cat_ref[h * dv:(h + 1) * dv, :] = (on * (1.0 - lam_init)).astype(BF16)

    o_ref[0] = x_ref[0] + _dot_tn(cat_ref[...], wo_ref[...])


def _da_attn_prompt(lams, q, kb, vt, x, wo, subln, n_heads, lam_init):
    b, t, d = x.shape
    dh = d // (2 * n_heads)
    tq = ATTN_TILE if t % ATTN_TILE == 0 else t
    assert tq % CHUNK == 0
    blk = pl.BlockSpec((1, tq, d), lambda bi, i: (bi, i, 0))
    seq = pl.BlockSpec((1, t, d), lambda bi, i: (bi, 0, 0), pipeline_mode=pl.Buffered(1))
    seq_t = pl.BlockSpec((d, t), lambda bi, i: (0, bi), pipeline_mode=pl.Buffered(1))
    lam_spec = _resident((1, dh))
    return pl.pallas_call(
        functools.partial(_da_attn_kernel, n_heads=n_heads, dh=dh, tq=tq, lam_init=lam_init),
        out_shape=jax.ShapeDtypeStruct((b, t, d), F32),
        grid=(b, t // tq),
        in_specs=[lam_spec] * 4 + [blk, seq, seq_t, blk, _resident((d, d)),
                                   _resident((2 * dh, 1))],
        out_specs=blk,
        scratch_shapes=[pltpu.VMEM((2 * n_heads, 2 * dh + SUM_ROWS, tq), F32),
                        pltpu.VMEM((d, tq), BF16)],
        compiler_params=_params("parallel", "arbitrary"),
        name="da_attn_prompt",
    )(*lams, q, kb, vt, x, wo, subln.reshape(2 * dh, 1))


def _da_attn_sample_kernel(lq1_ref, lk1_ref, lq2_ref, lk2_ref, q_ref, kc_ref, vc_ref, kn_ref,
                           vn_ref, x_ref, wo_ref, sub_ref, o_ref, cat_ref, *, n_heads, dh,
                           lam_init):
    lam = _da_lambda(lq1_ref, lk1_ref, lq2_ref, lk2_ref, lam_init)
    rows_per_token = 2 * n_heads
    past = kc_ref.shape[1] // rows_per_token

    def cached(ref, row):
        return ref.at[0][pl.ds(row, past, stride=rows_per_token), :].astype(BF16)

    scores = []
    for chain in range(2 * n_heads):
        cols = slice(chain * dh, (chain + 1) * dh)
        qh = q_ref[0, :, cols]
        scores.append((_dot_nt(qh, cached(kc_ref, chain)), _dot_nt(qh, kn_ref[0, :, cols])))
    probs = []
    for sc, sn in scores:
        m = jnp.maximum(jnp.max(sc, axis=1, keepdims=True), jnp.max(sn, axis=1, keepdims=True))
        pc = jnp.exp2(sc - m)
        pn = jnp.exp2(sn - m)
        l = jnp.sum(pc, axis=1, keepdims=True) + jnp.sum(pn, axis=1, keepdims=True)
        probs.append((pc / l, pn / l))
    for h in range(n_heads):
        c0 = h * 2 * dh
        ac = (probs[2 * h][0] - lam * probs[2 * h + 1][0]).astype(BF16)
        an = (probs[2 * h][1] - lam * probs[2 * h + 1][1]).astype(BF16)
        vc = jnp.concatenate([cached(vc_ref, h), cached(vc_ref, n_heads + h)], axis=1)
        o = _dot(ac, vc) + _dot(an, vn_ref[0, :, c0:c0 + 2 * dh])
        cat_ref[:, c0:c0 + 2 * dh] = _da_head_out(o, sub_ref, lam_init)
    o_ref[0] = x_ref[0] + _dot(cat_ref[...], wo_ref[...])


def _da_attn_sample(lams, q, k_rows, v_rows, kb, vb, x, wo, subln, n_heads, lam_init):
    b, ts, d = x.shape
    dh = d // (2 * n_heads)
    past = k_rows.shape[1] // (2 * n_heads)
    assert past % CHUNK == 0 and ts <= CHUNK
    new = pl.BlockSpec((1, ts, d), lambda bi: (bi, 0, 0))
    cache = pl.BlockSpec((1, past * 2 * n_heads, dh), lambda bi: (bi, 0, 0))
    lam_spec = _resident((1, dh))
    return pl.pallas_call(
        functools.partial(_da_attn_sample_kernel, n_heads=n_heads, dh=dh, lam_init=lam_init),
        out_shape=jax.ShapeDtypeStruct((b, ts, d), F32),
        grid=(b,),
        in_specs=[lam_spec] * 4 + [new, cache, cache, new, new, new, _resident((d, d)),
                                   _resident((1, 2 * dh))],
        out_specs=new,
        scratch_shapes=[pltpu.VMEM((ts, d), BF16)],
        compiler_params=_params("parallel"),
        name="da_attn_sample",
    )(*lams, q, k_rows, v_rows, kb, vb, x, wo, subln.reshape(1, 2 * dh))


def _rope_tables(pos, dk):
    angle = 1.0 / (ROPE_BASE ** jnp.linspace(0.0, 1.0, dk // 2, dtype=F32))
    angle = jnp.repeat(angle, 2)
    theta = pos[:, None] * angle[None, :]
    sign = jnp.where(jnp.arange(dk) % 2 == 0, -1.0, 1.0).astype(F32)
    return jnp.cos(theta), jnp.sin(theta) * sign[None, :]


def _ret_proj_kernel(x_ref, g_ref, w_ref, cos_ref, sin_ref, q_ref, k_ref, v_ref, gate_ref, *,
                     n_heads, dk, k_scale):
    nq = n_heads * dk
    nv = v_ref.shape[1]
    h = _rmsnorm(x_ref[...], g_ref[...], NORM_EPS).astype(BF16)
    cos = cos_ref[...]
    sin = sin_ref[...]
    even = lax.broadcasted_iota(jnp.int32, cos.shape, 1) % 2 == 0

    def rotary(a):
        rot = jnp.where(even, pltpu.roll(a, dk - 1, 1), pltpu.roll(a, 1, 1))
        return a * cos + rot * sin

    q = _dot(h, w_ref[:, 0:nq])
    k = _dot(h, w_ref[:, nq:2 * nq])
    for hd in range(n_heads):
        sl = slice(hd * dk, (hd + 1) * dk)
        q_ref[:, sl] = rotary(q[:, sl]).astype(BF16)
        k_ref[:, sl] = (rotary(k[:, sl]) * k_scale).astype(BF16)
    v_ref[...] = _dot(h, w_ref[:, 2 * nq:2 * nq + nv]).astype(BF16)
    gate_ref[...] = _dot(h, w_ref[:, 2 * nq + nv:2 * nq + 2 * nv])


def _ret_proj(x, g, w, cos, sin, n_heads, dk, dv):
    n, d = x.shape
    nq, nv = n_heads * dk, n_heads * dv
    tm = _row_tile(cos.shape[0])
    pos_blocks = cos.shape[0] // tm
    row = lambda width: pl.BlockSpec((tm, width), lambda i: (i, 0))
    tab = pl.BlockSpec((tm, dk), lambda i: (i % pos_blocks, 0))
    return pl.pallas_call(
        functools.partial(_ret_proj_kernel, n_heads=n_heads, dk=dk, k_scale=dk ** -0.5),
        out_shape=(jax.ShapeDtypeStruct((n, nq), BF16), jax.ShapeDtypeStruct((n, nq), BF16),
                   jax.ShapeDtypeStruct((n, nv), BF16), jax.ShapeDtypeStruct((n, nv), F32)),
        grid=(n // tm,),
        in_specs=[row(d), _resident((1, d)), _resident((d, 2 * nq + 2 * nv)), tab, tab],
        out_specs=(row(nq), row(nq), row(nv), row(nv)),
        compiler_params=_params("parallel"),
        name="ret_proj",
    )(x, g.reshape(1, d), w, cos, sin)


def _ret_core_kernel(lg_ref, q_ref, k_ref, v_ref, gate_ref, x_ref, s0_ref, wo_ref, o_ref,
                     sfin_ref, s_scr, y_scr, decay_scr, *, n_heads, dk, dv, blk):
    c = pl.program_id(1)

    row = lax.broadcasted_iota(jnp.int32, (blk, blk), 0)
    col = lax.broadcasted_iota(jnp.int32, (blk, blk), 1)
    rel = (row - col).astype(F32)
    idx = lax.broadcasted_iota(jnp.int32, (blk, 1), 0).astype(F32)

    @pl.when(c == 0)
    def _():
        s_scr[...] = s0_ref[0]
        for h in range(n_heads):
            decay_scr[h] = jnp.where(rel >= 0, jnp.exp(jnp.maximum(rel, 0.0) * lg_ref[h]), 0.0)

    first = []
    for h in range(n_heads):
        lg = lg_ref[h]
        qh = q_ref[0, :, h * dk:(h + 1) * dk]
        kh = k_ref[0, :, h * dk:(h + 1) * dk]
        vh = v_ref[0, :, h * dv:(h + 1) * dv]
        s = s_scr[h]
        qk = _dot_nt(qh, kh)
        cross = _dot(qh, s.astype(BF16))
        kd = (kh.astype(F32) * jnp.exp((blk - 1.0 - idx) * lg)).astype(BF16)
        s_decay = jnp.exp(jnp.full((1, dv), float(blk), F32) * lg)
        s_scr[h] = s * s_decay + _dot_tn(kd, vh)
        first.append((qk, cross))

    for h, (qk, cross) in enumerate(first):
        vh = v_ref[0, :, h * dv:(h + 1) * dv]
        inner = _dot((qk * decay_scr[h]).astype(BF16), vh)
        o = inner + cross * jnp.exp((idx + 1.0) * lg_ref[h])
        on = o * lax.rsqrt(jnp.mean(o * o, axis=-1, keepdims=True) + RET_GN_EPS)
        gate = gate_ref[0, :, h * dv:(h + 1) * dv]
        y_scr[:, h * dv:(h + 1) * dv] = (gate * jax.nn.sigmoid(gate) * on).astype(BF16)

    o_ref[0] = x_ref[0] + _dot(y_scr[...], wo_ref[...])

    @pl.when(c == pl.num_programs(1) - 1)
    def _():
        sfin_ref[0] = s_scr[...]


def _ret_core(lg, q, k, v, gate, x, s0, wo, n_heads, dk, dv):
    b, t, d = x.shape
    blk = RET_TILE if t % RET_TILE == 0 else t
    nq, nv = n_heads * dk, n_heads * dv
    tok = lambda width: pl.BlockSpec((1, blk, width), lambda bi, c: (bi, c, 0))
    state = pl.BlockSpec((1, n_heads, dk, dv), lambda bi, c: (bi, 0, 0, 0))
    return pl.pallas_call(
        functools.partial(_ret_core_kernel, n_heads=n_heads, dk=dk, dv=dv, blk=blk),
        out_shape=(jax.ShapeDtypeStruct((b, t, d), F32),
                   jax.ShapeDtypeStruct((b, n_heads, dk, dv), F32)),
        grid=(b, t // blk),
        in_specs=[pl.BlockSpec(memory_space=pltpu.SMEM), tok(nq), tok(nq), tok(nv), tok(nv),
                  tok(d), state, _resident((nv, d))],
        out_specs=(tok(d), state),
        scratch_shapes=[pltpu.VMEM((n_heads, dk, dv), F32), pltpu.VMEM((blk, nv), BF16),
                        pltpu.VMEM((n_heads, blk, blk), F32)],
        compiler_params=_params("parallel", "arbitrary"),
        name="ret_core",
    )(lg, q, k, v, gate, x, s0, wo)


def _lambda_init(layer_idx):
    return 0.8 - 0.6 * math.exp(-0.3 * layer_idx)


def kernel(x_prompt, x_sample, cache_diff_k, cache_diff_v, state_ret, ffn1_norm, ffn1_w_gate, ffn1_w_up, ffn1_w_down, mix_norm, da_w_qkv, da_lambda_q1, da_lambda_k1, da_lambda_q2, da_lambda_k2, da_subln, da_w_o, ret_w_in, ret_w_o, ffn2_norm, ffn2_w_gate, ffn2_w_up, ffn2_w_down, final_norm):
    bp, tp, d = x_prompt.shape
    bs, ts, _ = x_sample.shape
    past = cache_diff_k.shape[2]
    da_heads = cache_diff_k.shape[3]
    dh = cache_diff_k.shape[5]
    ret_heads, dk, dv = state_ret.shape[2:]
    depth = ffn1_norm.shape[0]
    bf = lambda w: w.astype(BF16)

    xp = x_prompt.reshape(bp * tp, d)
    xs = x_sample.reshape(bs * ts, d)
    kp_list, vp_list, sp_list = [], [], []
    ks_list, vs_list, ss_list = [], [], []
    for i in range(depth):
        xp, *w1 = _ffn(xp, ffn1_norm[i], ffn1_w_gate, ffn1_w_up, ffn1_w_down, layer=i)
        xs = _ffn(xs, ffn1_norm[i], *w1)
        if i % 2 == 0:
            a = i // 2
            lam_init = _lambda_init(i)
            w_o = bf(da_w_o[a])
            lams = [v[a].reshape(1, dh) for v in (da_lambda_q1, da_lambda_k1, da_lambda_q2, da_lambda_k2)]
            q_scale = dh ** -0.5 * math.log2(math.e)
            qp, kp, vp, kpb, vpt, w_qkv = _da_proj(xp, mix_norm[i], da_w_qkv, q_scale, True,
                                                   da_heads, layer=a)
            qs, ks, vs, ksb, vsb = _da_proj(xs, mix_norm[i], w_qkv, q_scale, False, da_heads)
            seq = lambda z, b, t: z.reshape(b, t, d)
            xp = _da_attn_prompt(lams, seq(qp, bp, tp), seq(kpb, bp, tp), vpt,
                                 seq(xp, bp, tp), w_o, da_subln[a], da_heads,
                                 lam_init).reshape(bp * tp, d)
            xs = _da_attn_sample(lams, seq(qs, bs, ts), _key_cache_to_rows(cache_diff_k[a]),
                                 _value_cache_to_rows(cache_diff_v[a]), seq(ksb, bs, ts),
                                 seq(vsb, bs, ts), seq(xs, bs, ts), w_o, da_subln[a], da_heads,
                                 lam_init).reshape(bs * ts, d)
            kp_list.append(_key_rows_to_cache(kp, bp, tp, da_heads, dh))
            vp_list.append(_value_rows_to_cache(vp, bp, tp, da_heads, dh))
            ks_list.append(_key_rows_to_cache(ks, bs, ts, da_heads, dh))
            vs_list.append(_value_rows_to_cache(vs, bs, ts, da_heads, dh))
        else:
            r = i // 2
            w_in, w_o = bf(ret_w_in[r]), bf(ret_w_o[r])
            lg = jnp.log1p(-jnp.exp2(-5.0 - jnp.arange(ret_heads, dtype=F32)))
            nq, nv = ret_heads * dk, ret_heads * dv
            groups = ((xp, bp, tp, jnp.arange(tp, dtype=F32),
                       jnp.zeros((bp, ret_heads, dk, dv), F32), sp_list),
                      (xs, bs, ts, past + jnp.arange(ts, dtype=F32),
                       state_ret[r].astype(F32), ss_list))
            outs = []
            for x, b, t, pos, s0, s_list in groups:
                cos, sin = _rope_tables(pos, dk)
                if t % ROW_TILE != 0:
                    cos, sin = jnp.tile(cos, (b, 1)), jnp.tile(sin, (b, 1))
                q, k, v, gate = _ret_proj(x, mix_norm[i], w_in, cos, sin, ret_heads, dk, dv)
                y, s_fin = _ret_core(lg, q.reshape(b, t, nq), k.reshape(b, t, nq),
                                     v.reshape(b, t, nv), gate.reshape(b, t, nv),
                                     x.reshape(b, t, d), s0, w_o, ret_heads, dk, dv)
                outs.append(y.reshape(b * t, d))
                s_list.append(s_fin)
            xp, xs = outs
        fin = final_norm if i == depth - 1 else None
        xp, *w2 = _ffn(xp, ffn2_norm[i], ffn2_w_gate, ffn2_w_up, ffn2_w_down, final_g=fin, layer=i)
        xs = _ffn(xs, ffn2_norm[i], *w2, final_g=fin)

    return (xp.reshape(bp, tp, d), xs.reshape(bs, ts, d),
            jnp.stack(kp_list), jnp.stack(vp_list), jnp.stack(sp_list),
            jnp.stack(ks_list), jnp.stack(vs_list), jnp.stack(ss_list))
```

```python
import functools
import math

import jax
import jax.numpy as jnp
import numpy as np
from jax import lax
from jax.experimental import pallas as pl
from jax.experimental.pallas import tpu as pltpu

F32 = jnp.float32
BF16 = jnp.bfloat16

NORM_EPS = 1e-6
DA_SUBLN_EPS = 1e-5
RET_GN_EPS = 1e-6
ROPE_BASE = 10000.0
CHUNK = 64
NEG_INF = -1e30

LANES = 128
V7X_VMEM_BYTES = 64 * 1024 * 1024
VMEM_LIMIT_BYTES = (V7X_VMEM_BYTES * 3) // 4
VMEM_LIMIT_ATTN_BYTES = (V7X_VMEM_BYTES * 7) // 8

ROW_TILE = 512
ATTN_TILE = 512
RET_TILE = 256
SCORE_LOOKAHEAD = 2
WEIGHT_STAGE_CHUNKS = 8
SUM_ROWS = 16


def _params(*semantics):
    return pltpu.CompilerParams(dimension_semantics=semantics,
                                vmem_limit_bytes=VMEM_LIMIT_BYTES)


def _resident(shape):
    return pl.BlockSpec(shape, lambda *_: (0,) * len(shape),
                        pipeline_mode=pl.Buffered(1))


def _rmsnorm(x, g, eps):
    return x * lax.rsqrt(jnp.mean(x * x, axis=-1, keepdims=True) + eps) * g


def _dot(a, b):
    return jnp.dot(a, b, preferred_element_type=F32)


def _dot_nt(a, b):
    return lax.dot_general(a, b, (((1,), (1,)), ((), ())), preferred_element_type=F32)


def _dot_tn(a, b):
    return lax.dot_general(a, b, (((0,), (0,)), ((), ())), preferred_element_type=F32)


def _row_tile(n):
    return ROW_TILE if n % ROW_TILE == 0 else n


def _stage_and_cast(src_hbm, dst_vmem, stage, sem):
    rows = stage.shape[1]
    n_chunks = src_hbm.shape[0] // rows

    def chunk_copy(c, slot):
        return pltpu.make_async_copy(src_hbm.at[pl.ds(c * rows, rows)], stage.at[slot], sem.at[slot])

    chunk_copy(0, 0).start()

    def body(c, carry):
        slot = lax.rem(c, 2)

        @pl.when(c + 1 < n_chunks)
        def _():
            chunk_copy(c + 1, 1 - slot).start()

        chunk_copy(c, slot).wait()
        dst_vmem[pl.ds(pl.multiple_of(c * rows, rows), rows), :] = stage[slot].astype(BF16)
        return carry

    lax.fori_loop(0, n_chunks, body, 0)


def _stage_shape(weight_shape):
    rows, cols = weight_shape
    assert rows % WEIGHT_STAGE_CHUNKS == 0
    return (2, rows // WEIGHT_STAGE_CHUNKS, cols)


def _ffn_body(x_ref, g_ref, wg_ref, wu_ref, wd_ref, fg_ref, o_ref, f_tile, before_chunk=None):
    x = x_ref[...]
    h = _rmsnorm(x, g_ref[...], NORM_EPS).astype(BF16)
    acc = None
    for c in range(wg_ref.shape[1] // f_tile):
        sl = slice(c * f_tile, (c + 1) * f_tile)
        if before_chunk is not None:
            before_chunk(c)
        gate = _dot(h, wg_ref[:, sl])
        up = _dot(h, wu_ref[:, sl])
        a = (gate * jax.nn.sigmoid(gate) * up).astype(BF16)
        d = _dot(a, wd_ref[sl, :])
        acc = d if acc is None else acc + d
    y = x + 0.5 * acc
    if fg_ref is not None:
        y = _rmsnorm(y, fg_ref[...], NORM_EPS)
    o_ref[...] = y


def _ffn_kernel(x_ref, g_ref, wg_ref, wu_ref, wd_ref, *rest, f_tile, final):
    fg_ref, o_ref = rest if final else (None,) + rest
    _ffn_body(x_ref, g_ref, wg_ref, wu_ref, wd_ref, fg_ref, o_ref, f_tile)


def _ffn_cast_kernel(x_ref, g_ref, wg_hbm, wu_hbm, wd_hbm, *rest, f_tile, final, layer):
    fg_ref = rest[0] if final else None
    o_ref, wg_out, wu_out, wd_out, wg_v, wu_v, wd_v, stage_up, stage_down, sem_in, sem_out = (
        rest[1:] if final else rest)
    i = pl.program_id(0)
    n_chunks = wg_v.shape[1] // f_tile
    copies_out = [pltpu.make_async_copy(v, out, sem_out.at[k])
                  for k, (v, out) in enumerate(((wg_v, wg_out), (wu_v, wu_out), (wd_v, wd_out)))]

    def chunk_copies(c):
        slot, cols = c % 2, pl.ds(c * f_tile, f_tile)
        return (pltpu.make_async_copy(wg_hbm.at[layer, :, cols], stage_up.at[slot, 0], sem_in.at[slot, 0]),
                pltpu.make_async_copy(wu_hbm.at[layer, :, cols], stage_up.at[slot, 1], sem_in.at[slot, 1]),
                pltpu.make_async_copy(wd_hbm.at[layer, cols, :], stage_down.at[slot], sem_in.at[slot, 2]))

    def stage_chunk(c):
        if c + 1 < n_chunks:
            for copy in chunk_copies(c + 1):
                copy.start()
        for copy in chunk_copies(c):
            copy.wait()
        slot, sl = c % 2, slice(c * f_tile, (c + 1) * f_tile)
        wg_v[:, sl] = stage_up[slot, 0].astype(BF16)
        wu_v[:, sl] = stage_up[slot, 1].astype(BF16)
        wd_v[sl, :] = stage_down[slot].astype(BF16)

    @pl.when(i == 0)
    def _():
        for copy in chunk_copies(0):
            copy.start()
        _ffn_body(x_ref, g_ref, wg_v, wu_v, wd_v, fg_ref, o_ref, f_tile, before_chunk=stage_chunk)
        for copy_out in copies_out:
            copy_out.start()

    @pl.when(i > 0)
    def _():
        _ffn_body(x_ref, g_ref, wg_v, wu_v, wd_v, fg_ref, o_ref, f_tile)

    @pl.when(i == pl.num_programs(0) - 1)
    def _():
        for copy_out in copies_out:
            copy_out.wait()


def _ffn(x, g, wg, wu, wd, final_g=None, layer=None):
    n, d = x.shape
    f = wg.shape[-1]
    tm = _row_tile(n)
    f_tile = 256 if f % 256 == 0 else f
    final = final_g is not None
    cast = layer is not None
    row = pl.BlockSpec((tm, d), lambda i: (i, 0))
    hbm = pl.BlockSpec(memory_space=pl.ANY)
    w_specs = [hbm] * 3 if cast else [_resident((d, f)), _resident((d, f)), _resident((f, d))]
    in_specs = [row, _resident((1, d))] + w_specs
    args = [x, g.reshape(1, d), wg, wu, wd]
    if final:
        in_specs.append(_resident((1, d)))
        args.append(final_g.reshape(1, d))
    y_shape = jax.ShapeDtypeStruct((n, d), F32)
    name = "ffn_final" if final else "ffn"
    if not cast:
        return pl.pallas_call(
            functools.partial(_ffn_kernel, f_tile=f_tile, final=final),
            out_shape=y_shape, grid=(n // tm,), in_specs=in_specs, out_specs=row,
            compiler_params=_params("parallel"), name=name,
        )(*args)
    up_shape, down_shape = wg.shape[1:], wd.shape[1:]
    bf16_like = lambda shape: jax.ShapeDtypeStruct(shape, BF16)
    return pl.pallas_call(
        functools.partial(_ffn_cast_kernel, f_tile=f_tile, final=final, layer=layer),
        out_shape=(y_shape, bf16_like(up_shape), bf16_like(up_shape), bf16_like(down_shape)),
        grid=(n // tm,),
        in_specs=in_specs,
        out_specs=(row, hbm, hbm, hbm),
        scratch_shapes=[pltpu.VMEM(up_shape, BF16), pltpu.VMEM(up_shape, BF16),
                        pltpu.VMEM(down_shape, BF16), pltpu.VMEM((2, 2, d, f_tile), F32),
                        pltpu.VMEM((2, f_tile, d), F32),
                        pltpu.SemaphoreType.DMA((2, 3)), pltpu.SemaphoreType.DMA((3,))],
        compiler_params=_params("arbitrary"),
        name=name + "_cast",
    )(*args)


def _da_proj_body(x_ref, g_ref, w_ref, q_ref, k_ref, v_ref, kb_ref, vb_ref, q_scale,
                  transposed_v, n_heads):
    tm, d = x_ref.shape
    lane_tiles = d // LANES
    h = _rmsnorm(x_ref[...], g_ref[...], NORM_EPS).astype(BF16)
    q_ref[...] = (_dot(h, w_ref[:, 0:d]) * q_scale).astype(BF16)
    k = _dot(h, w_ref[:, d:2 * d])
    kb_ref[...] = k.astype(BF16)
    v = _dot(h, w_ref[:, 2 * d:3 * d])
    for j in range(lane_tiles):
        cols = slice(j * LANES, (j + 1) * LANES)
        k_ref[pl.ds(j, tm, stride=lane_tiles), :] = k[:, cols]
        head, half = divmod(j, lane_tiles // n_heads)
        v_ref[pl.ds(half * n_heads + head, tm, stride=lane_tiles), :] = v[:, cols]
    vb_ref[...] = (v.T if transposed_v else v).astype(BF16)


def _da_proj_kernel(x_ref, g_ref, w_ref, *outs, **static):
    _da_proj_body(x_ref, g_ref, w_ref, *outs, **static)


def _da_proj_cast_kernel(x_ref, g_ref, w_hbm, q_ref, k_ref, v_ref, kb_ref, vb_ref, w_out, w_v,
                         stage, sem_in, sem_out, *, layer, **static):
    i = pl.program_id(0)
    copy_out = pltpu.make_async_copy(w_v, w_out, sem_out.at[0])

    @pl.when(i == 0)
    def _():
        _stage_and_cast(w_hbm.at[layer], w_v, stage, sem_in)
        copy_out.start()

    _da_proj_body(x_ref, g_ref, w_v, q_ref, k_ref, v_ref, kb_ref, vb_ref, **static)

    @pl.when(i == pl.num_programs(0) - 1)
    def _():
        copy_out.wait()


def _da_proj(x, g, w, q_scale, transposed_v, n_heads, layer=None):
    n, d = x.shape
    tm = _row_tile(n)
    lane_tiles = d // LANES
    cast = layer is not None
    row = pl.BlockSpec((tm, d), lambda i: (i, 0))
    rows = pl.BlockSpec((tm * lane_tiles, LANES), lambda i: (i, 0))
    hbm = pl.BlockSpec(memory_space=pl.ANY)
    if transposed_v:
        vb_shape, vb_spec = (d, n), pl.BlockSpec((d, tm), lambda i: (0, i))
    else:
        vb_shape, vb_spec = (n, d), row
    out_shape = [jax.ShapeDtypeStruct((n, d), BF16),
                 jax.ShapeDtypeStruct((n * lane_tiles, LANES), F32),
                 jax.ShapeDtypeStruct((n * lane_tiles, LANES), F32),
                 jax.ShapeDtypeStruct((n, d), BF16),
                 jax.ShapeDtypeStruct(vb_shape, BF16)]
    out_specs = [row, rows, rows, row, vb_spec]
    static = dict(q_scale=q_scale, transposed_v=transposed_v, n_heads=n_heads)
    name = "da_proj_t" if transposed_v else "da_proj"
    if not cast:
        return pl.pallas_call(
            functools.partial(_da_proj_kernel, **static),
            out_shape=out_shape, grid=(n // tm,),
            in_specs=[row, _resident((1, d)), _resident((d, 3 * d))], out_specs=out_specs,
            compiler_params=_params("parallel"), name=name,
        )(x, g.reshape(1, d), w)
    w_shape = w.shape[1:]
    return pl.pallas_call(
        functools.partial(_da_proj_cast_kernel, layer=layer, **static),
        out_shape=out_shape + [jax.ShapeDtypeStruct(w_shape, BF16)], grid=(n // tm,),
        in_specs=[row, _resident((1, d)), hbm], out_specs=out_specs + [hbm],
        scratch_shapes=[pltpu.VMEM(w_shape, BF16), pltpu.VMEM(_stage_shape(w_shape), F32),
                        pltpu.SemaphoreType.DMA((2,)), pltpu.SemaphoreType.DMA((1,))],
        compiler_params=_params("arbitrary"), name=name + "_cast",
    )(x, g.reshape(1, d), w)


def _key_rows_to_cache(k_rows, b, t, n_heads, dh):
    return k_rows.reshape(b, t, n_heads, 2, dh)


def _value_rows_to_cache(v_rows, b, t, n_heads, dh):
    v = v_rows.reshape(b, t, 2, n_heads, dh)
    return jnp.swapaxes(v, 2, 3).reshape(b, t, n_heads, 2 * dh)


def _key_cache_to_rows(k_cache):
    b, t, n_heads, _, dh = k_cache.shape
    return k_cache.reshape(b, t * n_heads * 2, dh)


def _value_cache_to_rows(v_cache):
    b, t, n_heads, dv = v_cache.shape
    v = v_cache.reshape(b, t, n_heads, 2, dv // 2)
    return jnp.swapaxes(v, 2, 3).reshape(b, t * 2 * n_heads, dv // 2)


def _da_lambda(lq1_ref, lk1_ref, lq2_ref, lk2_ref, lam_init):
    s1 = jnp.sum(lq1_ref[...] * lk1_ref[...], axis=1, keepdims=True)
    s2 = jnp.sum(lq2_ref[...] * lk2_ref[...], axis=1, keepdims=True)
    return jnp.exp(s1) - jnp.exp(s2) + lam_init


def _da_head_out(o, sub_ref, lam_init):
    return (_rmsnorm(o, sub_ref[...], DA_SUBLN_EPS) * (1.0 - lam_init)).astype(BF16)


def _da_attn_kernel(lq1_ref, lk1_ref, lq2_ref, lk2_ref, q_ref, k_ref, vt_ref, x_ref, wo_ref,
                    sub_ref, o_ref, acc_ref, cat_ref, *, n_heads, dh, tq, lam_init):
    i = pl.program_id(1)
    dv = 2 * dh
    qn = tq // 2
    n_chains = 2 * n_heads
    lam = _da_lambda(lq1_ref, lk1_ref, lq2_ref, lk2_ref, lam_init)
    key_chunk = lax.broadcasted_iota(jnp.int32, (qn, qn), 0) // CHUNK
    qry_chunk = lax.broadcasted_iota(jnp.int32, (qn, qn), 1) // CHUNK
    diag_mask = key_chunk <= qry_chunk
    ones = jnp.ones((SUM_ROWS, tq), BF16)

    acc_ref[...] = jnp.zeros_like(acc_ref)

    def run_pieces(pieces, stats):
        stats = dict(stats)

        def scores(piece):
            chain, q0, qw, k0, kn, _ = piece
            cols = slice(chain * dh, (chain + 1) * dh)
            return _dot_nt(k_ref[0, pl.ds(k0, kn), cols], q_ref[0, q0:q0 + qw, cols])

        pending = [scores(p) for p in pieces[:SCORE_LOOKAHEAD]]
        for n, (chain, q0, qw, k0, kn, masked) in enumerate(pieces):
            s = pending.pop(0)
            if n + SCORE_LOOKAHEAD < len(pieces):
                pending.append(scores(pieces[n + SCORE_LOOKAHEAD]))
            if masked == qw:
                s = jnp.where(diag_mask, s, NEG_INF)
            elif masked:
                s = jnp.concatenate([jnp.where(diag_mask, s[:, :masked], NEG_INF), s[:, masked:]],
                                    axis=1)
            m_old = stats[chain, q0]
            m_new = jnp.maximum(m_old, jnp.max(s, axis=0, keepdims=True))
            alpha = jnp.exp2(m_old - m_new)
            p = jnp.exp2(s - m_new).astype(BF16)
            h = chain // 2
            vt = jnp.concatenate([vt_ref[h * dv:(h + 1) * dv, pl.ds(k0, kn)], ones[:, :kn]], axis=0)
            lanes = slice(q0, q0 + qw)
            acc_ref[chain, :, lanes] = alpha * acc_ref[chain, :, lanes] + _dot(vt, p)
            stats[chain, q0] = m_new
        return stats

    def full_block(j, maxima):
        k0 = pl.multiple_of(j * tq, tq)
        stats = run_pieces([(c, 0, tq, k0, tq, 0) for c in range(n_chains)],
                           {(c, 0): maxima[c] for c in range(n_chains)})
        return tuple(stats[c, 0] for c in range(n_chains))

    m0 = jnp.full((1, tq), NEG_INF, F32)
    maxima = lax.fori_loop(0, i, full_block, (m0,) * n_chains)

    k_diag = pl.multiple_of(i * tq, tq)
    stats = run_pieces([(c, 0, tq, k_diag, qn, qn) for c in range(n_chains)],
                       {(c, 0): maxima[c] for c in range(n_chains)})
    run_pieces([(c, qn, qn, k_diag + qn, qn, qn) for c in range(n_chains)],
               {(c, qn): stats[c, 0][:, qn:] for c in range(n_chains)})

    for h in range(n_heads):
        l1, l2 = acc_ref[2 * h, dv:dv + 1, :], acc_ref[2 * h + 1, dv:dv + 1, :]
        o = acc_ref[2 * h, :dv, :] / l1 - lam * (acc_ref[2 * h + 1, :dv, :] / l2)
        on = o * lax.rsqrt(jnp.mean(o * o, axis=0, keepdims=True) + DA_SUBLN_EPS) * sub_ref[...]
        cat_ref[h * dv:(h + 1) * dv, :] = (on * (1.0 - lam_init)).astype(BF16)

    o_ref[0] = x_ref[0] + _dot_tn(cat_ref[...], wo_ref[...])


def _da_attn_prompt(lams, q, kb, vt, x, wo, subln, n_heads, lam_init):
    b, t, d = x.shape
    dh = d // (2 * n_heads)
    tq = ATTN_TILE if t % ATTN_TILE == 0 else t
    assert tq % CHUNK == 0
    blk = pl.BlockSpec((1, tq, d), lambda bi, i: (bi, i, 0))
    seq = pl.BlockSpec((1, t, d), lambda bi, i: (bi, 0, 0))
    seq_t = pl.BlockSpec((d, t), lambda bi, i: (0, bi))
    lam_spec = _resident((1, dh))
    return pl.pallas_call(
        functools.partial(_da_attn_kernel, n_heads=n_heads, dh=dh, tq=tq, lam_init=lam_init),
        out_shape=jax.ShapeDtypeStruct((b, t, d), F32),
        grid=(b, t // tq),
        in_specs=[lam_spec] * 4 + [blk, seq, seq_t, blk, _resident((d, d)),
                                   _resident((2 * dh, 1))],
        out_specs=blk,
        scratch_shapes=[pltpu.VMEM((2 * n_heads, 2 * dh + SUM_ROWS, tq), F32),
                        pltpu.VMEM((d, tq), BF16)],
        compiler_params=pltpu.CompilerParams(dimension_semantics=("parallel", "arbitrary"),
                                             vmem_limit_bytes=VMEM_LIMIT_ATTN_BYTES),
        name="da_attn_prompt",
    )(*lams, q, kb, vt, x, wo, subln.reshape(2 * dh, 1))


def _da_attn_sample_kernel(lq1_ref, lk1_ref, lq2_ref, lk2_ref, q_ref, kc_ref, vc_ref, kn_ref,
                           vn_ref, x_ref, wo_ref, sub_ref, o_ref, cat_ref, *, n_heads, dh,
                           lam_init):
    lam = _da_lambda(lq1_ref, lk1_ref, lq2_ref, lk2_ref, lam_init)
    rows_per_token = 2 * n_heads
    past = kc_ref.shape[1] // rows_per_token

    def cached(ref, row):
        return ref.at[0][pl.ds(row, past, stride=rows_per_token), :].astype(BF16)

    scores = []
    for chain in range(2 * n_heads):
        cols = slice(chain * dh, (chain + 1) * dh)
        qh = q_ref[0, :, cols]
        scores.append((_dot_nt(qh, cached(kc_ref, chain)), _dot_nt(qh, kn_ref[0, :, cols])))
    probs = []
    for sc, sn in scores:
        m = jnp.maximum(jnp.max(sc, axis=1, keepdims=True), jnp.max(sn, axis=1, keepdims=True))
        pc = jnp.exp2(sc - m)
        pn = jnp.exp2(sn - m)
        l = jnp.sum(pc, axis=1, keepdims=True) + jnp.sum(pn, axis=1, keepdims=True)
        probs.append((pc / l, pn / l))
    for h in range(n_heads):
        c0 = h * 2 * dh
        ac = (probs[2 * h][0] - lam * probs[2 * h + 1][0]).astype(BF16)
        an = (probs[2 * h][1] - lam * probs[2 * h + 1][1]).astype(BF16)
        vc = jnp.concatenate([cached(vc_ref, h), cached(vc_ref, n_heads + h)], axis=1)
        o = _dot(ac, vc) + _dot(an, vn_ref[0, :, c0:c0 + 2 * dh])
        cat_ref[:, c0:c0 + 2 * dh] = _da_head_out(o, sub_ref, lam_init)
    o_ref[0] = x_ref[0] + _dot(cat_ref[...], wo_ref[...])


def _da_attn_sample(lams, q, k_rows, v_rows, kb, vb, x, wo, subln, n_heads, lam_init):
    b, ts, d = x.shape
    dh = d // (2 * n_heads)
    past = k_rows.shape[1] // (2 * n_heads)
    assert past % CHUNK == 0 and ts <= CHUNK
    new = pl.BlockSpec((1, ts, d), lambda bi: (bi, 0, 0))
    cache = pl.BlockSpec((1, past * 2 * n_heads, dh), lambda bi: (bi, 0, 0))
    lam_spec = _resident((1, dh))
    return pl.pallas_call(
        functools.partial(_da_attn_sample_kernel, n_heads=n_heads, dh=dh, lam_init=lam_init),
        out_shape=jax.ShapeDtypeStruct((b, ts, d), F32),
        grid=(b,),
        in_specs=[lam_spec] * 4 + [new, cache, cache, new, new, new, _resident((d, d)),
                                   _resident((1, 2 * dh))],
        out_specs=new,
        scratch_shapes=[pltpu.VMEM((ts, d), BF16)],
        compiler_params=_params("parallel"),
        name="da_attn_sample",
    )(*lams, q, k_rows, v_rows, kb, vb, x, wo, subln.reshape(1, 2 * dh))


def _rope_tables(first_pos, n_pos, dk):
    angle = np.repeat(1.0 / (ROPE_BASE ** np.linspace(0.0, 1.0, dk // 2)), 2)
    theta = (first_pos + np.arange(n_pos, dtype=np.float64))[:, None] * angle[None, :]
    sign = np.where(np.arange(dk) % 2 == 0, -1.0, 1.0)
    return jnp.asarray(np.cos(theta), F32), jnp.asarray(np.sin(theta) * sign[None, :], F32)


def _ret_proj_kernel(x_ref, g_ref, w_ref, cos_ref, sin_ref, q_ref, k_ref, v_ref, gate_ref, *,
                     n_heads, dk, k_scale):
    nq = n_heads * dk
    nv = v_ref.shape[1]
    h = _rmsnorm(x_ref[...], g_ref[...], NORM_EPS).astype(BF16)
    cos = cos_ref[...]
    sin = sin_ref[...]
    even = lax.broadcasted_iota(jnp.int32, cos.shape, 1) % 2 == 0

    def rotary(a):
        rot = jnp.where(even, pltpu.roll(a, dk - 1, 1), pltpu.roll(a, 1, 1))
        return a * cos + rot * sin

    q = _dot(h, w_ref[:, 0:nq])
    k = _dot(h, w_ref[:, nq:2 * nq])
    for hd in range(n_heads):
        sl = slice(hd * dk, (hd + 1) * dk)
        q_ref[:, sl] = rotary(q[:, sl]).astype(BF16)
        k_ref[:, sl] = (rotary(k[:, sl]) * k_scale).astype(BF16)
    v_ref[...] = _dot(h, w_ref[:, 2 * nq:2 * nq + nv]).astype(BF16)
    gate_ref[...] = _dot(h, w_ref[:, 2 * nq + nv:2 * nq + 2 * nv])


def _ret_proj(x, g, w, cos, sin, n_heads, dk, dv):
    n, d = x.shape
    nq, nv = n_heads * dk, n_heads * dv
    tm = _row_tile(cos.shape[0])
    pos_blocks = cos.shape[0] // tm
    row = lambda width: pl.BlockSpec((tm, width), lambda i: (i, 0))
    tab = pl.BlockSpec((tm, dk), lambda i: (i % pos_blocks, 0))
    return pl.pallas_call(
        functools.partial(_ret_proj_kernel, n_heads=n_heads, dk=dk, k_scale=dk ** -0.5),
        out_shape=(jax.ShapeDtypeStruct((n, nq), BF16), jax.ShapeDtypeStruct((n, nq), BF16),
                   jax.ShapeDtypeStruct((n, nv), BF16), jax.ShapeDtypeStruct((n, nv), F32)),
        grid=(n // tm,),
        in_specs=[row(d), _resident((1, d)), _resident((d, 2 * nq + 2 * nv)), tab, tab],
        out_specs=(row(nq), row(nq), row(nv), row(nv)),
        compiler_params=_params("parallel"),
        name="ret_proj",
    )(x, g.reshape(1, d), w, cos, sin)


def _ret_core_kernel(lg_ref, q_ref, k_ref, v_ref, gate_ref, x_ref, *rest, n_heads, dk, dv, blk,
                     has_state):
    s0_ref = rest[0] if has_state else None
    wo_ref, o_ref, sfin_ref, s_scr, y_scr, decay_scr = rest[1:] if has_state else rest
    c = pl.program_id(1)

    row = lax.broadcasted_iota(jnp.int32, (blk, blk), 0)
    col = lax.broadcasted_iota(jnp.int32, (blk, blk), 1)
    rel = (row - col).astype(F32)
    idx = lax.broadcasted_iota(jnp.int32, (blk, 1), 0).astype(F32)

    @pl.when(c == 0)
    def _():
        s_scr[...] = s0_ref[0] if has_state else jnp.zeros_like(s_scr)
        for h in range(n_heads):
            decay_scr[h] = jnp.where(rel >= 0, jnp.exp(jnp.maximum(rel, 0.0) * lg_ref[h]), 0.0)

    first = []
    for h in range(n_heads):
        lg = lg_ref[h]
        qh = q_ref[0, :, h * dk:(h + 1) * dk]
        kh = k_ref[0, :, h * dk:(h + 1) * dk]
        vh = v_ref[0, :, h * dv:(h + 1) * dv]
        s = s_scr[h]
        qk = _dot_nt(qh, kh)
        cross = _dot(qh, s.astype(BF16))
        kd = (kh.astype(F32) * jnp.exp((blk - 1.0 - idx) * lg)).astype(BF16)
        s_decay = jnp.exp(jnp.full((1, dv), float(blk), F32) * lg)
        s_scr[h] = s * s_decay + _dot_tn(kd, vh)
        first.append((qk, cross))

    for h, (qk, cross) in enumerate(first):
        vh = v_ref[0, :, h * dv:(h + 1) * dv]
        inner = _dot((qk * decay_scr[h]).astype(BF16), vh)
        o = inner + cross * jnp.exp((idx + 1.0) * lg_ref[h])
        on = o * lax.rsqrt(jnp.mean(o * o, axis=-1, keepdims=True) + RET_GN_EPS)
        gate = gate_ref[0, :, h * dv:(h + 1) * dv]
        y_scr[:, h * dv:(h + 1) * dv] = (gate * jax.nn.sigmoid(gate) * on).astype(BF16)

    o_ref[0] = x_ref[0] + _dot(y_scr[...], wo_ref[...])

    @pl.when(c == pl.num_programs(1) - 1)
    def _():
        sfin_ref[0] = s_scr[...]


def _ret_core(lg, q, k, v, gate, x, s0, wo, n_heads, dk, dv):
    b, t, d = x.shape
    blk = RET_TILE if t % RET_TILE == 0 else t
    nq, nv = n_heads * dk, n_heads * dv
    tok = lambda width: pl.BlockSpec((1, blk, width), lambda bi, c: (bi, c, 0))
    state = pl.BlockSpec((1, n_heads, dk, dv), lambda bi, c: (bi, 0, 0, 0))
    has_state = s0 is not None
    return pl.pallas_call(
        functools.partial(_ret_core_kernel, n_heads=n_heads, dk=dk, dv=dv, blk=blk,
                          has_state=has_state),
        out_shape=(jax.ShapeDtypeStruct((b, t, d), F32),
                   jax.ShapeDtypeStruct((b, n_heads, dk, dv), F32)),
        grid=(b, t // blk),
        in_specs=[pl.BlockSpec(memory_space=pltpu.SMEM), tok(nq), tok(nq), tok(nv), tok(nv),
                  tok(d)] + [state] * has_state + [_resident((nv, d))],
        out_specs=(tok(d), state),
        scratch_shapes=[pltpu.VMEM((n_heads, dk, dv), F32), pltpu.VMEM((blk, nv), BF16),
                        pltpu.VMEM((n_heads, blk, blk), F32)],
        compiler_params=_params("parallel", "arbitrary"),
        name="ret_core",
    )(lg, q, k, v, gate, x, *([s0] if has_state else []), wo)


def _lambda_init(layer_idx):
    return 0.8 - 0.6 * math.exp(-0.3 * layer_idx)


def kernel(x_prompt, x_sample, cache_diff_k, cache_diff_v, state_ret, ffn1_norm, ffn1_w_gate, ffn1_w_up, ffn1_w_down, mix_norm, da_w_qkv, da_lambda_q1, da_lambda_k1, da_lambda_q2, da_lambda_k2, da_subln, da_w_o, ret_w_in, ret_w_o, ffn2_norm, ffn2_w_gate, ffn2_w_up, ffn2_w_down, final_norm):
    bp, tp, d = x_prompt.shape
    bs, ts, _ = x_sample.shape
    past = cache_diff_k.shape[2]
    da_heads = cache_diff_k.shape[3]
    dh = cache_diff_k.shape[5]
    ret_heads, dk, dv = state_ret.shape[2:]
    depth = ffn1_norm.shape[0]
    bf = lambda w: w.astype(BF16)

    xp = x_prompt.reshape(bp * tp, d)
    xs = x_sample.reshape(bs * ts, d)
    kp_list, vp_list, sp_list = [], [], []
    ks_list, vs_list, ss_list = [], [], []
    for i in range(depth):
        xp, *w1 = _ffn(xp, ffn1_norm[i], ffn1_w_gate, ffn1_w_up, ffn1_w_down, layer=i)
        xs = _ffn(xs, ffn1_norm[i], *w1)
        if i % 2 == 0:
            a = i // 2
            lam_init = _lambda_init(i)
            w_o = bf(da_w_o[a])
            lams = [v[a].reshape(1, dh) for v in (da_lambda_q1, da_lambda_k1, da_lambda_q2, da_lambda_k2)]
            q_scale = dh ** -0.5 * math.log2(math.e)
            qp, kp, vp, kpb, vpt, w_qkv = _da_proj(xp, mix_norm[i], da_w_qkv, q_scale, True,
                                                   da_heads, layer=a)
            qs, ks, vs, ksb, vsb = _da_proj(xs, mix_norm[i], w_qkv, q_scale, False, da_heads)
            seq = lambda z, b, t: z.reshape(b, t, d)
            xp = _da_attn_prompt(lams, seq(qp, bp, tp), seq(kpb, bp, tp), vpt,
                                 seq(xp, bp, tp), w_o, da_subln[a], da_heads,
                                 lam_init).reshape(bp * tp, d)
            xs = _da_attn_sample(lams, seq(qs, bs, ts), _key_cache_to_rows(cache_diff_k[a]),
                                 _value_cache_to_rows(cache_diff_v[a]), seq(ksb, bs, ts),
                                 seq(vsb, bs, ts), seq(xs, bs, ts), w_o, da_subln[a], da_heads,
                                 lam_init).reshape(bs * ts, d)
            kp_list.append(_key_rows_to_cache(kp, bp, tp, da_heads, dh))
            vp_list.append(_value_rows_to_cache(vp, bp, tp, da_heads, dh))
            ks_list.append(_key_rows_to_cache(ks, bs, ts, da_heads, dh))
            vs_list.append(_value_rows_to_cache(vs, bs, ts, da_heads, dh))
        else:
            r = i // 2
            w_in, w_o = bf(ret_w_in[r]), bf(ret_w_o[r])
            lg = jnp.log1p(-jnp.exp2(-5.0 - jnp.arange(ret_heads, dtype=F32)))
            nq, nv = ret_heads * dk, ret_heads * dv
            groups = ((xp, bp, tp, 0, None, sp_list),
                      (xs, bs, ts, past, state_ret[r].astype(F32), ss_list))
            outs = []
            for x, b, t, first_pos, s0, s_list in groups:
                cos, sin = _rope_tables(first_pos, t, dk)
                if t % ROW_TILE != 0:
                    cos, sin = jnp.tile(cos, (b, 1)), jnp.tile(sin, (b, 1))
                q, k, v, gate = _ret_proj(x, mix_norm[i], w_in, cos, sin, ret_heads, dk, dv)
                y, s_fin = _ret_core(lg, q.reshape(b, t, nq), k.reshape(b, t, nq),
                                     v.reshape(b, t, nv), gate.reshape(b, t, nv),
                                     x.reshape(b, t, d), s0, w_o, ret_heads, dk, dv)
                outs.append(y.reshape(b * t, d))
                s_list.append(s_fin)
            xp, xs = outs
        fin = final_norm if i == depth - 1 else None
        xp, *w2 = _ffn(xp, ffn2_norm[i], ffn2_w_gate, ffn2_w_up, ffn2_w_down, final_g=fin, layer=i)
        xs = _ffn(xs, ffn2_norm[i], *w2, final_g=fin)

    return (xp.reshape(bp, tp, d), xs.reshape(bs, ts, d),
            jnp.stack(kp_list), jnp.stack(vp_list), jnp.stack(sp_list),
            jnp.stack(ks_list), jnp.stack(vs_list), jnp.stack(ss_list))
```

```python
import functools
import math

import jax
import jax.numpy as jnp
import numpy as np
from jax import lax
from jax.experimental import pallas as pl
from jax.experimental.pallas import tpu as pltpu

F32 = jnp.float32
BF16 = jnp.bfloat16

NORM_EPS = 1e-6
DA_SUBLN_EPS = 1e-5
RET_GN_EPS = 1e-6
ROPE_BASE = 10000.0
CHUNK = 64
NEG_INF = -1e30

LANES = 128
MXU_TILE = 256
V7X_VMEM_BYTES = 64 * 1024 * 1024
VMEM_LIMIT_BYTES = (V7X_VMEM_BYTES * 3) // 4
VMEM_LIMIT_LARGE_BYTES = (V7X_VMEM_BYTES * 7) // 8

ROW_TILE = 512
ATTN_TILE = 512
RET_TILE = 256
SCORE_LOOKAHEAD = 2
WEIGHT_STAGE_CHUNKS = 8
SUM_ROWS = 16


def _params(*semantics):
    return pltpu.CompilerParams(dimension_semantics=semantics,
                                vmem_limit_bytes=VMEM_LIMIT_BYTES)


def _resident(shape):
    return pl.BlockSpec(shape, lambda *_: (0,) * len(shape),
                        pipeline_mode=pl.Buffered(1))


def _rmsnorm(x, g, eps):
    return x * lax.rsqrt(jnp.mean(x * x, axis=-1, keepdims=True) + eps) * g


def _dot(a, b):
    return jnp.dot(a, b, preferred_element_type=F32)


def _dot_nt(a, b):
    return lax.dot_general(a, b, (((1,), (1,)), ((), ())), preferred_element_type=F32)


def _dot_tn(a, b):
    return lax.dot_general(a, b, (((0,), (0,)), ((), ())), preferred_element_type=F32)


def _row_tile(n):
    return ROW_TILE if n % ROW_TILE == 0 else n


def _stage_and_cast(src_hbm, dst_vmem, stage, sem):
    rows = stage.shape[1]
    n_chunks = src_hbm.shape[0] // rows

    def chunk_copy(c, slot):
        return pltpu.make_async_copy(src_hbm.at[pl.ds(c * rows, rows)], stage.at[slot], sem.at[slot])

    chunk_copy(0, 0).start()

    def body(c, carry):
        slot = lax.rem(c, 2)

        @pl.when(c + 1 < n_chunks)
        def _():
            chunk_copy(c + 1, 1 - slot).start()

        chunk_copy(c, slot).wait()
        dst_vmem[pl.ds(pl.multiple_of(c * rows, rows), rows), :] = stage[slot].astype(BF16)
        return carry

    lax.fori_loop(0, n_chunks, body, 0)


def _stage_shape(weight_shape):
    rows, cols = weight_shape
    assert rows % WEIGHT_STAGE_CHUNKS == 0
    return (2, rows // WEIGHT_STAGE_CHUNKS, cols)


def _ffn_rows(x, g_ref, wg_ref, wu_ref, wd_ref, fg_ref, f_tile, before_chunk=None):
    h = _rmsnorm(x, g_ref[...], NORM_EPS).astype(BF16)
    acc = None
    for c in range(wg_ref.shape[1] // f_tile):
        sl = slice(c * f_tile, (c + 1) * f_tile)
        if before_chunk is not None:
            before_chunk(c)
        gate = _dot(h, wg_ref[:, sl])
        up = _dot(h, wu_ref[:, sl])
        a = (gate * jax.nn.sigmoid(gate) * up).astype(BF16)
        d = _dot(a, wd_ref[sl, :])
        acc = d if acc is None else acc + d
    y = x + 0.5 * acc
    if fg_ref is not None:
        y = _rmsnorm(y, fg_ref[...], NORM_EPS)
    return y


def _ffn_kernel(x_ref, xe_ref, g_ref, wg_hbm, wu_hbm, wd_hbm, *rest, f_tile, final, layer):
    fg_ref = rest[0] if final else None
    o_ref, oe_ref, wg_v, wu_v, wd_v, stage_up, stage_down, sem_in = rest[1:] if final else rest
    i = pl.program_id(0)
    last = pl.num_programs(0) - 1
    n_chunks = wg_v.shape[1] // f_tile

    def chunk_copies(c):
        slot, cols = c % 2, pl.ds(c * f_tile, f_tile)
        return (pltpu.make_async_copy(wg_hbm.at[layer, :, cols], stage_up.at[slot, 0], sem_in.at[slot, 0]),
                pltpu.make_async_copy(wu_hbm.at[layer, :, cols], stage_up.at[slot, 1], sem_in.at[slot, 1]),
                pltpu.make_async_copy(wd_hbm.at[layer, cols, :], stage_down.at[slot], sem_in.at[slot, 2]))

    def stage_chunk(c):
        if c + 1 < n_chunks:
            for copy in chunk_copies(c + 1):
                copy.start()
        for copy in chunk_copies(c):
            copy.wait()
        slot, sl = c % 2, slice(c * f_tile, (c + 1) * f_tile)
        wg_v[:, sl] = stage_up[slot, 0].astype(BF16)
        wu_v[:, sl] = stage_up[slot, 1].astype(BF16)
        wd_v[sl, :] = stage_down[slot].astype(BF16)

    def run(rows_ref, out_ref, before_chunk=None):
        out_ref[...] = _ffn_rows(rows_ref[...], g_ref, wg_v, wu_v, wd_v, fg_ref, f_tile,
                                 before_chunk)

    @pl.when(i == 0)
    def _():
        for copy in chunk_copies(0):
            copy.start()
        run(x_ref, o_ref, stage_chunk)

    @pl.when(jnp.logical_and(i > 0, i < last))
    def _():
        run(x_ref, o_ref)

    @pl.when(i == last)
    def _():
        run(xe_ref, oe_ref)


def _ffn(x, x_extra, g, wg, wu, wd, layer, final_g=None):
    n, d = x.shape
    f = wg.shape[-1]
    tm = _row_tile(n)
    assert x_extra.shape == (tm, d)
    n_tiles = n // tm
    f_tile = MXU_TILE if f % MXU_TILE == 0 else f
    final = final_g is not None
    row = pl.BlockSpec((tm, d), lambda i: (jnp.minimum(i, n_tiles - 1), 0))
    extra = pl.BlockSpec((tm, d), lambda i: (0, 0))
    hbm = pl.BlockSpec(memory_space=pl.ANY)
    in_specs = [row, _resident((tm, d)), _resident((1, d)), hbm, hbm, hbm]
    args = [x, x_extra, g.reshape(1, d), wg, wu, wd]
    if final:
        in_specs.append(_resident((1, d)))
        args.append(final_g.reshape(1, d))
    up_shape, down_shape = wg.shape[1:], wd.shape[1:]
    return pl.pallas_call(
        functools.partial(_ffn_kernel, f_tile=f_tile, final=final, layer=layer),
        out_shape=(jax.ShapeDtypeStruct((n, d), F32), jax.ShapeDtypeStruct((tm, d), F32)),
        grid=(n_tiles + 1,),
        in_specs=in_specs,
        out_specs=(row, extra),
        scratch_shapes=[pltpu.VMEM(up_shape, BF16), pltpu.VMEM(up_shape, BF16),
                        pltpu.VMEM(down_shape, BF16), pltpu.VMEM((2, 2, d, f_tile), F32),
                        pltpu.VMEM((2, f_tile, d), F32), pltpu.SemaphoreType.DMA((2, 3))],
        compiler_params=pltpu.CompilerParams(dimension_semantics=("arbitrary",),
                                             vmem_limit_bytes=VMEM_LIMIT_LARGE_BYTES),
        name="ffn_final" if final else "ffn",
    )(*args)


def _da_proj_body(x_ref, g_ref, w_ref, q_ref, k_ref, v_ref, kb_ref, vb_ref, q_scale,
                  transposed_v, n_heads):
    tm, d = x_ref.shape
    lane_tiles = d // LANES
    h = _rmsnorm(x_ref[...], g_ref[...], NORM_EPS).astype(BF16)
    q_ref[...] = (_dot(h, w_ref[:, 0:d]) * q_scale).astype(BF16)
    k = _dot(h, w_ref[:, d:2 * d])
    kb_ref[...] = k.astype(BF16)
    v = _dot(h, w_ref[:, 2 * d:3 * d])
    for j in range(lane_tiles):
        cols = slice(j * LANES, (j + 1) * LANES)
        k_ref[pl.ds(j, tm, stride=lane_tiles), :] = k[:, cols]
        head, half = divmod(j, lane_tiles // n_heads)
        v_ref[pl.ds(half * n_heads + head, tm, stride=lane_tiles), :] = v[:, cols]
    vb_ref[...] = (v.T if transposed_v else v).astype(BF16)


def _da_proj_kernel(x_ref, g_ref, w_ref, *outs, **static):
    _da_proj_body(x_ref, g_ref, w_ref, *outs, **static)


def _da_proj_cast_kernel(x_ref, g_ref, w_hbm, q_ref, k_ref, v_ref, kb_ref, vb_ref, w_out, w_v,
                         stage, sem_in, sem_out, *, layer, **static):
    i = pl.program_id(0)
    copy_out = pltpu.make_async_copy(w_v, w_out, sem_out.at[0])

    @pl.when(i == 0)
    def _():
        _stage_and_cast(w_hbm.at[layer], w_v, stage, sem_in)
        copy_out.start()

    _da_proj_body(x_ref, g_ref, w_v, q_ref, k_ref, v_ref, kb_ref, vb_ref, **static)

    @pl.when(i == pl.num_programs(0) - 1)
    def _():
        copy_out.wait()


def _da_proj(x, g, w, q_scale, transposed_v, n_heads, layer=None):
    n, d = x.shape
    tm = _row_tile(n)
    lane_tiles = d // LANES
    cast = layer is not None
    row = pl.BlockSpec((tm, d), lambda i: (i, 0))
    rows = pl.BlockSpec((tm * lane_tiles, LANES), lambda i: (i, 0))
    hbm = pl.BlockSpec(memory_space=pl.ANY)
    if transposed_v:
        vb_shape, vb_spec = (d, n), pl.BlockSpec((d, tm), lambda i: (0, i))
    else:
        vb_shape, vb_spec = (n, d), row
    out_shape = [jax.ShapeDtypeStruct((n, d), BF16),
                 jax.ShapeDtypeStruct((n * lane_tiles, LANES), F32),
                 jax.ShapeDtypeStruct((n * lane_tiles, LANES), F32),
                 jax.ShapeDtypeStruct((n, d), BF16),
                 jax.ShapeDtypeStruct(vb_shape, BF16)]
    out_specs = [row, rows, rows, row, vb_spec]
    static = dict(q_scale=q_scale, transposed_v=transposed_v, n_heads=n_heads)
    name = "da_proj_t" if transposed_v else "da_proj"
    if not cast:
        return pl.pallas_call(
            functools.partial(_da_proj_kernel, **static),
            out_shape=out_shape, grid=(n // tm,),
            in_specs=[row, _resident((1, d)), _resident((d, 3 * d))], out_specs=out_specs,
            compiler_params=_params("parallel"), name=name,
        )(x, g.reshape(1, d), w)
    w_shape = w.shape[1:]
    return pl.pallas_call(
        functools.partial(_da_proj_cast_kernel, layer=layer, **static),
        out_shape=out_shape + [jax.ShapeDtypeStruct(w_shape, BF16)], grid=(n // tm,),
        in_specs=[row, _resident((1, d)), hbm], out_specs=out_specs + [hbm],
        scratch_shapes=[pltpu.VMEM(w_shape, BF16), pltpu.VMEM(_stage_shape(w_shape), F32),
                        pltpu.SemaphoreType.DMA((2,)), pltpu.SemaphoreType.DMA((1,))],
        compiler_params=_params("arbitrary"), name=name + "_cast",
    )(x, g.reshape(1, d), w)


def _key_rows_to_cache(k_rows, b, t, n_heads, dh):
    return k_rows.reshape(b, t, n_heads, 2, dh)


def _value_rows_to_cache(v_rows, b, t, n_heads, dh):
    v = v_rows.reshape(b, t, 2, n_heads, dh)
    return jnp.swapaxes(v, 2, 3).reshape(b, t, n_heads, 2 * dh)


def _key_cache_to_rows(k_cache):
    b, t, n_heads, _, dh = k_cache.shape
    return k_cache.reshape(b, t * n_heads * 2, dh)


def _value_cache_to_rows(v_cache):
    b, t, n_heads, dv = v_cache.shape
    v = v_cache.reshape(b, t, n_heads, 2, dv // 2)
    return jnp.swapaxes(v, 2, 3).reshape(b, t * 2 * n_heads, dv // 2)


def _da_lambda(lq1_ref, lk1_ref, lq2_ref, lk2_ref, lam_init):
    s1 = jnp.sum(lq1_ref[...] * lk1_ref[...], axis=1, keepdims=True)
    s2 = jnp.sum(lq2_ref[...] * lk2_ref[...], axis=1, keepdims=True)
    return jnp.exp(s1) - jnp.exp(s2) + lam_init


def _da_head_out(o, sub_ref, lam_init):
    return (_rmsnorm(o, sub_ref[...], DA_SUBLN_EPS) * (1.0 - lam_init)).astype(BF16)


def _da_attn_kernel(lq1_ref, lk1_ref, lq2_ref, lk2_ref, q_ref, k_ref, vt_ref, x_ref, wo_ref,
                    sub_ref, o_ref, acc_ref, cat_ref, *, n_heads, dh, tq, lam_init):
    i = pl.program_id(1)
    dv = 2 * dh
    qn = tq // 2
    n_chains = 2 * n_heads
    lam = _da_lambda(lq1_ref, lk1_ref, lq2_ref, lk2_ref, lam_init)
    key_chunk = lax.broadcasted_iota(jnp.int32, (qn, qn), 0) // CHUNK
    qry_chunk = lax.broadcasted_iota(jnp.int32, (qn, qn), 1) // CHUNK
    diag_mask = key_chunk <= qry_chunk
    ones = jnp.ones((SUM_ROWS, tq), BF16)

    acc_ref[...] = jnp.zeros_like(acc_ref)

    def run_pieces(pieces, stats):
        stats = dict(stats)

        def scores(piece):
            chain, q0, qw, k0, kn, _ = piece
            cols = slice(chain * dh, (chain + 1) * dh)
            return _dot_nt(k_ref[0, pl.ds(k0, kn), cols], q_ref[0, q0:q0 + qw, cols])

        pending = [scores(p) for p in pieces[:SCORE_LOOKAHEAD]]
        for n, (chain, q0, qw, k0, kn, masked) in enumerate(pieces):
            s = pending.pop(0)
            if n + SCORE_LOOKAHEAD < len(pieces):
                pending.append(scores(pieces[n + SCORE_LOOKAHEAD]))
            if masked == qw:
                s = jnp.where(diag_mask, s, NEG_INF)
            elif masked:
                s = jnp.concatenate([jnp.where(diag_mask, s[:, :masked], NEG_INF), s[:, masked:]],
                                    axis=1)
            m_old = stats[chain, q0]
            m_new = jnp.maximum(m_old, jnp.max(s, axis=0, keepdims=True))
            alpha = jnp.exp2(m_old - m_new)
            p = jnp.exp2(s - m_new).astype(BF16)
            h = chain // 2
            vt = jnp.concatenate([vt_ref[h * dv:(h + 1) * dv, pl.ds(k0, kn)], ones[:, :kn]], axis=0)
            lanes = slice(q0, q0 + qw)
            acc_ref[chain, :, lanes] = alpha * acc_ref[chain, :, lanes] + _dot(vt, p)
            stats[chain, q0] = m_new
        return stats

    def full_block(j, maxima):
        k0 = pl.multiple_of(j * tq, tq)
        stats = run_pieces([(c, 0, tq, k0, tq, 0) for c in range(n_chains)],
                           {(c, 0): maxima[c] for c in range(n_chains)})
        return tuple(stats[c, 0] for c in range(n_chains))

    m0 = jnp.full((1, tq), NEG_INF, F32)
    maxima = lax.fori_loop(0, i, full_block, (m0,) * n_chains)

    k_diag = pl.multiple_of(i * tq, tq)
    stats = run_pieces([(c, 0, tq, k_diag, qn, qn) for c in range(n_chains)],
                       {(c, 0): maxima[c] for c in range(n_chains)})
    run_pieces([(c, qn, qn, k_diag + qn, qn, qn) for c in range(n_chains)],
               {(c, qn): stats[c, 0][:, qn:] for c in range(n_chains)})

    for h in range(n_heads):
        l1, l2 = acc_ref[2 * h, dv:dv + 1, :], acc_ref[2 * h + 1, dv:dv + 1, :]
        o = acc_ref[2 * h, :dv, :] / l1 - lam * (acc_ref[2 * h + 1, :dv, :] / l2)
        on = o * lax.rsqrt(jnp.mean(o * o, axis=0, keepdims=True) + DA_SUBLN_EPS) * sub_ref[...]
        cat_ref[h * dv:(h + 1) * dv, :] = (on * (1.0 - lam_init)).astype(BF16)

    o_ref[0] = x_ref[0] + _dot_tn(cat_ref[...], wo_ref[...])


def _da_attn_prompt(lams, q, kb, vt, x, wo, subln, n_heads, lam_init):
    b, t, d = x.shape
    dh = d // (2 * n_heads)
    tq = ATTN_TILE if t % ATTN_TILE == 0 else t
    assert tq % CHUNK == 0
    blk = pl.BlockSpec((1, tq, d), lambda bi, i: (bi, i, 0))
    seq = pl.BlockSpec((1, t, d), lambda bi, i: (bi, 0, 0))
    seq_t = pl.BlockSpec((d, t), lambda bi, i: (0, bi))
    lam_spec = _resident((1, dh))
    return pl.pallas_call(
        functools.partial(_da_attn_kernel, n_heads=n_heads, dh=dh, tq=tq, lam_init=lam_init),
        out_shape=jax.ShapeDtypeStruct((b, t, d), F32),
        grid=(b, t // tq),
        in_specs=[lam_spec] * 4 + [blk, seq, seq_t, blk, _resident((d, d)),
                                   _resident((2 * dh, 1))],
        out_specs=blk,
        scratch_shapes=[pltpu.VMEM((2 * n_heads, 2 * dh + SUM_ROWS, tq), F32),
                        pltpu.VMEM((d, tq), BF16)],
        compiler_params=pltpu.CompilerParams(dimension_semantics=("parallel", "arbitrary"),
                                             vmem_limit_bytes=VMEM_LIMIT_LARGE_BYTES),
        name="da_attn_prompt",
    )(*lams, q, kb, vt, x, wo, subln.reshape(2 * dh, 1))


def _da_attn_sample_kernel(lq1_ref, lk1_ref, lq2_ref, lk2_ref, q_ref, kc_ref, vc_ref, kn_ref,
                           vn_ref, x_ref, wo_ref, sub_ref, o_ref, cat_ref, *, n_heads, dh,
                           lam_init):
    lam = _da_lambda(lq1_ref, lk1_ref, lq2_ref, lk2_ref, lam_init)
    rows_per_token = 2 * n_heads
    past = kc_ref.shape[1] // rows_per_token

    def cached(ref, row):
        return ref.at[0][pl.ds(row, past, stride=rows_per_token), :].astype(BF16)

    scores = []
    for chain in range(2 * n_heads):
        cols = slice(chain * dh, (chain + 1) * dh)
        qh = q_ref[0, :, cols]
        scores.append((_dot_nt(qh, cached(kc_ref, chain)), _dot_nt(qh, kn_ref[0, :, cols])))
    probs = []
    for sc, sn in scores:
        m = jnp.maximum(jnp.max(sc, axis=1, keepdims=True), jnp.max(sn, axis=1, keepdims=True))
        pc = jnp.exp2(sc - m)
        pn = jnp.exp2(sn - m)
        l = jnp.sum(pc, axis=1, keepdims=True) + jnp.sum(pn, axis=1, keepdims=True)
        probs.append((pc / l, pn / l))
    for h in range(n_heads):
        c0 = h * 2 * dh
        ac = (probs[2 * h][0] - lam * probs[2 * h + 1][0]).astype(BF16)
        an = (probs[2 * h][1] - lam * probs[2 * h + 1][1]).astype(BF16)
        vc = jnp.concatenate([cached(vc_ref, h), cached(vc_ref, n_heads + h)], axis=1)
        o = _dot(ac, vc) + _dot(an, vn_ref[0, :, c0:c0 + 2 * dh])
        cat_ref[:, c0:c0 + 2 * dh] = _da_head_out(o, sub_ref, lam_init)
    o_ref[0] = x_ref[0] + _dot(cat_ref[...], wo_ref[...])


def _da_attn_sample(lams, q, k_rows, v_rows, kb, vb, x, wo, subln, n_heads, lam_init):
    b, ts, d = x.shape
    dh = d // (2 * n_heads)
    past = k_rows.shape[1] // (2 * n_heads)
    assert past % CHUNK == 0 and ts <= CHUNK
    new = pl.BlockSpec((1, ts, d), lambda bi: (bi, 0, 0))
    cache = pl.BlockSpec((1, past * 2 * n_heads, dh), lambda bi: (bi, 0, 0))
    lam_spec = _resident((1, dh))
    return pl.pallas_call(
        functools.partial(_da_attn_sample_kernel, n_heads=n_heads, dh=dh, lam_init=lam_init),
        out_shape=jax.ShapeDtypeStruct((b, ts, d), F32),
        grid=(b,),
        in_specs=[lam_spec] * 4 + [new, cache, cache, new, new, new, _resident((d, d)),
                                   _resident((1, 2 * dh))],
        out_specs=new,
        scratch_shapes=[pltpu.VMEM((ts, d), BF16)],
        compiler_params=_params("parallel"),
        name="da_attn_sample",
    )(*lams, q, k_rows, v_rows, kb, vb, x, wo, subln.reshape(1, 2 * dh))


def _rope_tables(first_pos, n_pos, dk):
    angle = np.repeat(1.0 / (ROPE_BASE ** np.linspace(0.0, 1.0, dk // 2)), 2)
    theta = (first_pos + np.arange(n_pos, dtype=np.float64))[:, None] * angle[None, :]
    sign = np.where(np.arange(dk) % 2 == 0, -1.0, 1.0)
    return jnp.asarray(np.cos(theta), F32), jnp.asarray(np.sin(theta) * sign[None, :], F32)


def _ret_proj_kernel(x_ref, g_ref, w_ref, cos_ref, sin_ref, q_ref, k_ref, v_ref, gate_ref, *,
                     n_heads, dk, k_scale):
    nq = n_heads * dk
    nv = v_ref.shape[1]
    h = _rmsnorm(x_ref[...], g_ref[...], NORM_EPS).astype(BF16)
    cos = cos_ref[...]
    sin = sin_ref[...]
    even = lax.broadcasted_iota(jnp.int32, cos.shape, 1) % 2 == 0

    def rotary(a):
        rot = jnp.where(even, pltpu.roll(a, dk - 1, 1), pltpu.roll(a, 1, 1))
        return a * cos + rot * sin

    q = _dot(h, w_ref[:, 0:nq])
    k = _dot(h, w_ref[:, nq:2 * nq])
    for hd in range(n_heads):
        sl = slice(hd * dk, (hd + 1) * dk)
        q_ref[:, sl] = rotary(q[:, sl]).astype(BF16)
        k_ref[:, sl] = (rotary(k[:, sl]) * k_scale).astype(BF16)
    v_ref[...] = _dot(h, w_ref[:, 2 * nq:2 * nq + nv]).astype(BF16)
    gate_ref[...] = _dot(h, w_ref[:, 2 * nq + nv:2 * nq + 2 * nv])


def _ret_proj(x, g, w, cos, sin, n_heads, dk, dv):
    n, d = x.shape
    nq, nv = n_heads * dk, n_heads * dv
    tm = _row_tile(cos.shape[0])
    pos_blocks = cos.shape[0] // tm
    row = lambda width: pl.BlockSpec((tm, width), lambda i: (i, 0))
    tab = pl.BlockSpec((tm, dk), lambda i: (i % pos_blocks, 0))
    return pl.pallas_call(
        functools.partial(_ret_proj_kernel, n_heads=n_heads, dk=dk, k_scale=dk ** -0.5),
        out_shape=(jax.ShapeDtypeStruct((n, nq), BF16), jax.ShapeDtypeStruct((n, nq), BF16),
                   jax.ShapeDtypeStruct((n, nv), BF16), jax.ShapeDtypeStruct((n, nv), F32)),
        grid=(n // tm,),
        in_specs=[row(d), _resident((1, d)), _resident((d, 2 * nq + 2 * nv)), tab, tab],
        out_specs=(row(nq), row(nq), row(nv), row(nv)),
        compiler_params=_params("parallel"),
        name="ret_proj",
    )(x, g.reshape(1, d), w, cos, sin)


def _ret_core_kernel(lg_ref, q_ref, k_ref, v_ref, gate_ref, x_ref, *rest, n_heads, dk, dv, blk,
                     has_state):
    s0_ref = rest[0] if has_state else None
    wo_ref, o_ref, sfin_ref, s_scr, y_scr, decay_scr = rest[1:] if has_state else rest
    heads = lambda ref, width: (lambda h: ref[0, :, h * width:(h + 1) * width])
    _ret_block(lg_ref, heads(q_ref, dk), heads(k_ref, dk), heads(v_ref, dv), heads(gate_ref, dv),
               x_ref, s0_ref, wo_ref, o_ref, sfin_ref, s_scr, y_scr, decay_scr, n_heads, dv, blk)


def _ret_block(lg_ref, q_of, k_of, v_of, gate_of, x_ref, s0_ref, wo_ref, o_ref, sfin_ref, s_scr,
               y_scr, decay_scr, n_heads, dv, blk, between_stages=None):
    c = pl.program_id(1)
    row = lax.broadcasted_iota(jnp.int32, (blk, blk), 0)
    col = lax.broadcasted_iota(jnp.int32, (blk, blk), 1)
    rel = (row - col).astype(F32)
    idx = lax.broadcasted_iota(jnp.int32, (blk, 1), 0).astype(F32)

    @pl.when(c == 0)
    def _():
        s_scr[...] = jnp.zeros_like(s_scr) if s0_ref is None else s0_ref[0]
        for h in range(n_heads):
            decay_scr[h] = jnp.where(rel >= 0, jnp.exp(jnp.maximum(rel, 0.0) * lg_ref[h]), 0.0)

    first = []
    for h in range(n_heads):
        lg = lg_ref[h]
        qh, kh, vh = q_of(h), k_of(h), v_of(h)
        s = s_scr[h]
        qk = _dot_nt(qh, kh)
        cross = _dot(qh, s.astype(BF16))
        kd = (kh.astype(F32) * jnp.exp((blk - 1.0 - idx) * lg)).astype(BF16)
        s_decay = jnp.exp(jnp.full((1, dv), float(blk), F32) * lg)
        s_scr[h] = s * s_decay + _dot_tn(kd, vh)
        first.append((qk, cross))

    if between_stages is not None:
        between_stages()

    for h, (qk, cross) in enumerate(first):
        inner = _dot((qk * decay_scr[h]).astype(BF16), v_of(h))
        o = inner + cross * jnp.exp((idx + 1.0) * lg_ref[h])
        on = o * lax.rsqrt(jnp.mean(o * o, axis=-1, keepdims=True) + RET_GN_EPS)
        gate = gate_of(h)
        y_scr[:, h * dv:(h + 1) * dv] = (gate * jax.nn.sigmoid(gate) * on).astype(BF16)

    o_ref[0] = x_ref[0] + _dot(y_scr[...], wo_ref[...])

    @pl.when(c == pl.num_programs(1) - 1)
    def _():
        sfin_ref[0] = s_scr[...]


def _ret_fused_kernel(lg_ref, x_ref, g_ref, win_ref, cos_ref, sin_ref, wo_ref, o_ref, sfin_ref,
                      s_scr, y_scr, decay_scr, qkv_scr, gate_scr, *, n_heads, dk, dv, blk, k_scale):
    nq, nv = n_heads * dk, n_heads * dv
    h_in = _rmsnorm(x_ref[0], g_ref[...], NORM_EPS).astype(BF16)
    cos, sin = cos_ref[...], sin_ref[...]
    even = lax.broadcasted_iota(jnp.int32, cos.shape, 1) % 2 == 0

    def rotary(a):
        rot = jnp.where(even, pltpu.roll(a, dk - 1, 1), pltpu.roll(a, 1, 1))
        return a * cos + rot * sin

    q = _dot(h_in, win_ref[:, 0:nq])
    k = _dot(h_in, win_ref[:, nq:2 * nq])
    for hd in range(n_heads):
        sl_v = slice(hd * dv, (hd + 1) * dv)
        qkv_scr[:, 2 * nq + hd * dv:2 * nq + (hd + 1) * dv] = _dot(
            h_in, win_ref[:, 2 * nq + hd * dv:2 * nq + (hd + 1) * dv]).astype(BF16)
        gate_scr[:, sl_v] = _dot(h_in, win_ref[:, 2 * nq + nv + hd * dv:2 * nq + nv + (hd + 1) * dv])
        sl = slice(hd * dk, (hd + 1) * dk)
        qkv_scr[:, sl] = rotary(q[:, sl]).astype(BF16)
        qkv_scr[:, nq + hd * dk:nq + (hd + 1) * dk] = (rotary(k[:, sl]) * k_scale).astype(BF16)

    _ret_block(lg_ref,
               lambda h: qkv_scr[:, h * dk:(h + 1) * dk],
               lambda h: qkv_scr[:, nq + h * dk:nq + (h + 1) * dk],
               lambda h: qkv_scr[:, 2 * nq + h * dv:2 * nq + (h + 1) * dv],
               lambda h: gate_scr[:, h * dv:(h + 1) * dv],
               x_ref, None, wo_ref, o_ref, sfin_ref, s_scr, y_scr, decay_scr, n_heads, dv, blk)


def _ret_fused(lg, x, g, w_in, cos, sin, wo, n_heads, dk, dv):
    b, t, d = x.shape
    blk = RET_TILE
    assert t % blk == 0
    nq, nv = n_heads * dk, n_heads * dv
    tok = pl.BlockSpec((1, blk, d), lambda bi, c: (bi, c, 0))
    tab = pl.BlockSpec((blk, dk), lambda bi, c: (c, 0))
    state = pl.BlockSpec((1, n_heads, dk, dv), lambda bi, c: (bi, 0, 0, 0))
    return pl.pallas_call(
        functools.partial(_ret_fused_kernel, n_heads=n_heads, dk=dk, dv=dv, blk=blk,
                          k_scale=dk ** -0.5),
        out_shape=(jax.ShapeDtypeStruct((b, t, d), F32),
                   jax.ShapeDtypeStruct((b, n_heads, dk, dv), F32)),
        grid=(b, t // blk),
        in_specs=[pl.BlockSpec(memory_space=pltpu.SMEM), tok, _resident((1, d)),
                  _resident((d, 2 * nq + 2 * nv)), tab, tab, _resident((nv, d))],
        out_specs=(tok, state),
        scratch_shapes=[pltpu.VMEM((n_heads, dk, dv), F32), pltpu.VMEM((blk, nv), BF16),
                        pltpu.VMEM((n_heads, blk, blk), F32),
                        pltpu.VMEM((blk, 2 * nq + nv), BF16), pltpu.VMEM((blk, nv), F32)],
        compiler_params=_params("parallel", "arbitrary"),
        name="ret_fused",
    )(lg, x, g.reshape(1, d), w_in, cos, sin, wo)


def _ret_core(lg, q, k, v, gate, x, s0, wo, n_heads, dk, dv):
    b, t, d = x.shape
    blk = RET_TILE if t % RET_TILE == 0 else t
    nq, nv = n_heads * dk, n_heads * dv
    tok = lambda width: pl.BlockSpec((1, blk, width), lambda bi, c: (bi, c, 0))
    state = pl.BlockSpec((1, n_heads, dk, dv), lambda bi, c: (bi, 0, 0, 0))
    has_state = s0 is not None
    return pl.pallas_call(
        functools.partial(_ret_core_kernel, n_heads=n_heads, dk=dk, dv=dv, blk=blk,
                          has_state=has_state),
        out_shape=(jax.ShapeDtypeStruct((b, t, d), F32),
                   jax.ShapeDtypeStruct((b, n_heads, dk, dv), F32)),
        grid=(b, t // blk),
        in_specs=[pl.BlockSpec(memory_space=pltpu.SMEM), tok(nq), tok(nq), tok(nv), tok(nv),
                  tok(d)] + [state] * has_state + [_resident((nv, d))],
        out_specs=(tok(d), state),
        scratch_shapes=[pltpu.VMEM((n_heads, dk, dv), F32), pltpu.VMEM((blk, nv), BF16),
                        pltpu.VMEM((n_heads, blk, blk), F32)],
        compiler_params=_params("parallel", "arbitrary"),
        name="ret_core",
    )(lg, q, k, v, gate, x, *([s0] if has_state else []), wo)


def _lambda_init(layer_idx):
    return 0.8 - 0.6 * math.exp(-0.3 * layer_idx)


def kernel(x_prompt, x_sample, cache_diff_k, cache_diff_v, state_ret, ffn1_norm, ffn1_w_gate, ffn1_w_up, ffn1_w_down, mix_norm, da_w_qkv, da_lambda_q1, da_lambda_k1, da_lambda_q2, da_lambda_k2, da_subln, da_w_o, ret_w_in, ret_w_o, ffn2_norm, ffn2_w_gate, ffn2_w_up, ffn2_w_down, final_norm):
    bp, tp, d = x_prompt.shape
    bs, ts, _ = x_sample.shape
    past = cache_diff_k.shape[2]
    da_heads = cache_diff_k.shape[3]
    dh = cache_diff_k.shape[5]
    ret_heads, dk, dv = state_ret.shape[2:]
    depth = ffn1_norm.shape[0]
    bf = lambda w: w.astype(BF16)

    xp = x_prompt.reshape(bp * tp, d)
    xs = x_sample.reshape(bs * ts, d)
    kp_list, vp_list, sp_list = [], [], []
    ks_list, vs_list, ss_list = [], [], []
    for i in range(depth):
        xp, xs = _ffn(xp, xs, ffn1_norm[i], ffn1_w_gate, ffn1_w_up, ffn1_w_down, layer=i)
        if i % 2 == 0:
            a = i // 2
            lam_init = _lambda_init(i)
            w_o = bf(da_w_o[a])
            lams = [v[a].reshape(1, dh) for v in (da_lambda_q1, da_lambda_k1, da_lambda_q2, da_lambda_k2)]
            q_scale = dh ** -0.5 * math.log2(math.e)
            qp, kp, vp, kpb, vpt, w_qkv = _da_proj(xp, mix_norm[i], da_w_qkv, q_scale, True,
                                                   da_heads, layer=a)
            qs, ks, vs, ksb, vsb = _da_proj(xs, mix_norm[i], w_qkv, q_scale, False, da_heads)
            seq = lambda z, b, t: z.reshape(b, t, d)
            xp = _da_attn_prompt(lams, seq(qp, bp, tp), seq(kpb, bp, tp), vpt,
                                 seq(xp, bp, tp), w_o, da_subln[a], da_heads,
                                 lam_init).reshape(bp * tp, d)
            xs = _da_attn_sample(lams, seq(qs, bs, ts), _key_cache_to_rows(cache_diff_k[a]),
                                 _value_cache_to_rows(cache_diff_v[a]), seq(ksb, bs, ts),
                                 seq(vsb, bs, ts), seq(xs, bs, ts), w_o, da_subln[a], da_heads,
                                 lam_init).reshape(bs * ts, d)
            kp_list.append(_key_rows_to_cache(kp, bp, tp, da_heads, dh))
            vp_list.append(_value_rows_to_cache(vp, bp, tp, da_heads, dh))
            ks_list.append(_key_rows_to_cache(ks, bs, ts, da_heads, dh))
            vs_list.append(_value_rows_to_cache(vs, bs, ts, da_heads, dh))
        else:
            r = i // 2
            w_in, w_o = bf(ret_w_in[r]), bf(ret_w_o[r])
            lg = jnp.log1p(-jnp.exp2(-5.0 - jnp.arange(ret_heads, dtype=F32)))
            nq, nv = ret_heads * dk, ret_heads * dv
            cos, sin = _rope_tables(0, tp, dk)
            yp, sp = _ret_fused(lg, xp.reshape(bp, tp, d), mix_norm[i], w_in, cos, sin, w_o,
                                ret_heads, dk, dv)
            xp = yp.reshape(bp * tp, d)
            sp_list.append(sp)
            cos, sin = _rope_tables(past, ts, dk)
            cos, sin = jnp.tile(cos, (bs, 1)), jnp.tile(sin, (bs, 1))
            q, k, v, gate = _ret_proj(xs, mix_norm[i], w_in, cos, sin, ret_heads, dk, dv)
            ys, ss = _ret_core(lg, q.reshape(bs, ts, nq), k.reshape(bs, ts, nq),
                               v.reshape(bs, ts, nv), gate.reshape(bs, ts, nv),
                               xs.reshape(bs, ts, d), state_ret[r].astype(F32), w_o, ret_heads,
                               dk, dv)
            xs = ys.reshape(bs * ts, d)
            ss_list.append(ss)
        fin = final_norm if i == depth - 1 else None
        xp, xs = _ffn(xp, xs, ffn2_norm[i], ffn2_w_gate, ffn2_w_up, ffn2_w_down, layer=i,
                      final_g=fin)

    return (xp.reshape(bp, tp, d), xs.reshape(bs, ts, d),
            jnp.stack(kp_list), jnp.stack(vp_list), jnp.stack(sp_list),
            jnp.stack(ks_list), jnp.stack(vs_list), jnp.stack(ss_list))
```

```python
import functools
import math

import jax
import jax.numpy as jnp
import numpy as np
from jax import lax
from jax.experimental import pallas as pl
from jax.experimental.pallas import tpu as pltpu

F32 = jnp.float32
BF16 = jnp.bfloat16

NORM_EPS = 1e-6
DA_SUBLN_EPS = 1e-5
RET_GN_EPS = 1e-6
ROPE_BASE = 10000.0
CHUNK = 64
NEG_INF = -1e30

LANES = 128
MXU_TILE = 256
V7X_VMEM_BYTES = 64 * 1024 * 1024
VMEM_LIMIT_BYTES = (V7X_VMEM_BYTES * 3) // 4
VMEM_LIMIT_LARGE_BYTES = (V7X_VMEM_BYTES * 7) // 8

ROW_TILE = 512
ATTN_TILE = 512
RET_TILE = 256
RET_FUSED_ROWS = 512
SCORE_LOOKAHEAD = 2
WEIGHT_STAGE_CHUNKS = 8
SUM_ROWS = 16


def _params(*semantics):
    return pltpu.CompilerParams(dimension_semantics=semantics,
                                vmem_limit_bytes=VMEM_LIMIT_BYTES)


def _resident(shape):
    return pl.BlockSpec(shape, lambda *_: (0,) * len(shape),
                        pipeline_mode=pl.Buffered(1))


def _rmsnorm(x, g, eps):
    return x * lax.rsqrt(jnp.mean(x * x, axis=-1, keepdims=True) + eps) * g


def _dot(a, b):
    return jnp.dot(a, b, preferred_element_type=F32)


def _dot_nt(a, b):
    return lax.dot_general(a, b, (((1,), (1,)), ((), ())), preferred_element_type=F32)


def _dot_tn(a, b):
    return lax.dot_general(a, b, (((0,), (0,)), ((), ())), preferred_element_type=F32)


def _row_tile(n):
    return ROW_TILE if n % ROW_TILE == 0 else n


def _stage_and_cast(src_hbm, dst_vmem, stage, sem):
    rows = stage.shape[1]
    n_chunks = src_hbm.shape[0] // rows

    def chunk_copy(c, slot):
        return pltpu.make_async_copy(src_hbm.at[pl.ds(c * rows, rows)], stage.at[slot], sem.at[slot])

    chunk_copy(0, 0).start()

    def body(c, carry):
        slot = lax.rem(c, 2)

        @pl.when(c + 1 < n_chunks)
        def _():
            chunk_copy(c + 1, 1 - slot).start()

        chunk_copy(c, slot).wait()
        dst_vmem[pl.ds(pl.multiple_of(c * rows, rows), rows), :] = stage[slot].astype(BF16)
        return carry

    lax.fori_loop(0, n_chunks, body, 0)


def _stage_shape(weight_shape):
    rows, cols = weight_shape
    assert rows % WEIGHT_STAGE_CHUNKS == 0
    return (2, rows // WEIGHT_STAGE_CHUNKS, cols)


def _ffn_rows(x, g_ref, wg_ref, wu_ref, wd_ref, fg_ref, f_tile, before_chunk=None):
    h = _rmsnorm(x, g_ref[...], NORM_EPS).astype(BF16)
    acc = None
    for c in range(wg_ref.shape[1] // f_tile):
        sl = slice(c * f_tile, (c + 1) * f_tile)
        if before_chunk is not None:
            before_chunk(c)
        gate = _dot(h, wg_ref[:, sl])
        up = _dot(h, wu_ref[:, sl])
        a = (gate * jax.nn.sigmoid(gate) * up).astype(BF16)
        d = _dot(a, wd_ref[sl, :])
        acc = d if acc is None else acc + d
    y = x + 0.5 * acc
    if fg_ref is not None:
        y = _rmsnorm(y, fg_ref[...], NORM_EPS)
    return y


def _ffn_kernel(x_ref, xe_ref, g_ref, wg_hbm, wu_hbm, wd_hbm, *rest, f_tile, final, layer):
    fg_ref = rest[0] if final else None
    o_ref, oe_ref, wg_v, wu_v, wd_v, stage_up, stage_down, sem_in = rest[1:] if final else rest
    i = pl.program_id(0)
    last = pl.num_programs(0) - 1
    n_chunks = wg_v.shape[1] // f_tile

    def chunk_copies(c):
        slot, cols = c % 2, pl.ds(c * f_tile, f_tile)
        return (pltpu.make_async_copy(wg_hbm.at[layer, :, cols], stage_up.at[slot, 0], sem_in.at[slot, 0]),
                pltpu.make_async_copy(wu_hbm.at[layer, :, cols], stage_up.at[slot, 1], sem_in.at[slot, 1]),
                pltpu.make_async_copy(wd_hbm.at[layer, cols, :], stage_down.at[slot], sem_in.at[slot, 2]))

    def stage_chunk(c):
        if c + 1 < n_chunks:
            for copy in chunk_copies(c + 1):
                copy.start()
        for copy in chunk_copies(c):
            copy.wait()
        slot, sl = c % 2, slice(c * f_tile, (c + 1) * f_tile)
        wg_v[:, sl] = stage_up[slot, 0].astype(BF16)
        wu_v[:, sl] = stage_up[slot, 1].astype(BF16)
        wd_v[sl, :] = stage_down[slot].astype(BF16)

    def run(rows_ref, out_ref, before_chunk=None):
        out_ref[...] = _ffn_rows(rows_ref[...], g_ref, wg_v, wu_v, wd_v, fg_ref, f_tile,
                                 before_chunk)

    @pl.when(i == 0)
    def _():
        for copy in chunk_copies(0):
            copy.start()
        run(x_ref, o_ref, stage_chunk)

    @pl.when(jnp.logical_and(i > 0, i < last))
    def _():
        run(x_ref, o_ref)

    @pl.when(i == last)
    def _():
        run(xe_ref, oe_ref)


def _ffn(x, x_extra, g, wg, wu, wd, layer, final_g=None):
    n, d = x.shape
    f = wg.shape[-1]
    tm = _row_tile(n)
    assert x_extra.shape == (tm, d)
    n_tiles = n // tm
    f_tile = MXU_TILE if f % MXU_TILE == 0 else f
    final = final_g is not None
    row = pl.BlockSpec((tm, d), lambda i: (jnp.minimum(i, n_tiles - 1), 0))
    extra = pl.BlockSpec((tm, d), lambda i: (0, 0))
    hbm = pl.BlockSpec(memory_space=pl.ANY)
    in_specs = [row, _resident((tm, d)), _resident((1, d)), hbm, hbm, hbm]
    args = [x, x_extra, g.reshape(1, d), wg, wu, wd]
    if final:
        in_specs.append(_resident((1, d)))
        args.append(final_g.reshape(1, d))
    up_shape, down_shape = wg.shape[1:], wd.shape[1:]
    return pl.pallas_call(
        functools.partial(_ffn_kernel, f_tile=f_tile, final=final, layer=layer),
        out_shape=(jax.ShapeDtypeStruct((n, d), F32), jax.ShapeDtypeStruct((tm, d), F32)),
        grid=(n_tiles + 1,),
        in_specs=in_specs,
        out_specs=(row, extra),
        scratch_shapes=[pltpu.VMEM(up_shape, BF16), pltpu.VMEM(up_shape, BF16),
                        pltpu.VMEM(down_shape, BF16), pltpu.VMEM((2, 2, d, f_tile), F32),
                        pltpu.VMEM((2, f_tile, d), F32), pltpu.SemaphoreType.DMA((2, 3))],
        compiler_params=pltpu.CompilerParams(dimension_semantics=("arbitrary",),
                                             vmem_limit_bytes=VMEM_LIMIT_LARGE_BYTES),
        name="ffn_final" if final else "ffn",
    )(*args)


def _da_proj_body(x_ref, g_ref, w_ref, q_ref, k_ref, v_ref, kb_ref, vb_ref, q_scale,
                  transposed_v, n_heads):
    tm, d = x_ref.shape
    lane_tiles = d // LANES
    h = _rmsnorm(x_ref[...], g_ref[...], NORM_EPS).astype(BF16)
    q_ref[...] = (_dot(h, w_ref[:, 0:d]) * q_scale).astype(BF16)
    k = _dot(h, w_ref[:, d:2 * d])
    kb_ref[...] = k.astype(BF16)
    v = _dot(h, w_ref[:, 2 * d:3 * d])
    for j in range(lane_tiles):
        cols = slice(j * LANES, (j + 1) * LANES)
        k_ref[pl.ds(j, tm, stride=lane_tiles), :] = k[:, cols]
        head, half = divmod(j, lane_tiles // n_heads)
        v_ref[pl.ds(half * n_heads + head, tm, stride=lane_tiles), :] = v[:, cols]
    vb_ref[...] = (v.T if transposed_v else v).astype(BF16)


def _da_proj_kernel(x_ref, g_ref, w_ref, *outs, **static):
    _da_proj_body(x_ref, g_ref, w_ref, *outs, **static)


def _da_proj_cast_kernel(x_ref, g_ref, w_hbm, q_ref, k_ref, v_ref, kb_ref, vb_ref, w_out, w_v,
                         stage, sem_in, sem_out, *, layer, **static):
    i = pl.program_id(0)
    copy_out = pltpu.make_async_copy(w_v, w_out, sem_out.at[0])

    @pl.when(i == 0)
    def _():
        _stage_and_cast(w_hbm.at[layer], w_v, stage, sem_in)
        copy_out.start()

    _da_proj_body(x_ref, g_ref, w_v, q_ref, k_ref, v_ref, kb_ref, vb_ref, **static)

    @pl.when(i == pl.num_programs(0) - 1)
    def _():
        copy_out.wait()


def _da_proj(x, g, w, q_scale, transposed_v, n_heads, layer=None):
    n, d = x.shape
    tm = _row_tile(n)
    lane_tiles = d // LANES
    cast = layer is not None
    row = pl.BlockSpec((tm, d), lambda i: (i, 0))
    rows = pl.BlockSpec((tm * lane_tiles, LANES), lambda i: (i, 0))
    hbm = pl.BlockSpec(memory_space=pl.ANY)
    if transposed_v:
        vb_shape, vb_spec = (d, n), pl.BlockSpec((d, tm), lambda i: (0, i))
    else:
        vb_shape, vb_spec = (n, d), row
    out_shape = [jax.ShapeDtypeStruct((n, d), BF16),
                 jax.ShapeDtypeStruct((n * lane_tiles, LANES), F32),
                 jax.ShapeDtypeStruct((n * lane_tiles, LANES), F32),
                 jax.ShapeDtypeStruct((n, d), BF16),
                 jax.ShapeDtypeStruct(vb_shape, BF16)]
    out_specs = [row, rows, rows, row, vb_spec]
    static = dict(q_scale=q_scale, transposed_v=transposed_v, n_heads=n_heads)
    name = "da_proj_t" if transposed_v else "da_proj"
    if not cast:
        return pl.pallas_call(
            functools.partial(_da_proj_kernel, **static),
            out_shape=out_shape, grid=(n // tm,),
            in_specs=[row, _resident((1, d)), _resident((d, 3 * d))], out_specs=out_specs,
            compiler_params=_params("parallel"), name=name,
        )(x, g.reshape(1, d), w)
    w_shape = w.shape[1:]
    return pl.pallas_call(
        functools.partial(_da_proj_cast_kernel, layer=layer, **static),
        out_shape=out_shape + [jax.ShapeDtypeStruct(w_shape, BF16)], grid=(n // tm,),
        in_specs=[row, _resident((1, d)), hbm], out_specs=out_specs + [hbm],
        scratch_shapes=[pltpu.VMEM(w_shape, BF16), pltpu.VMEM(_stage_shape(w_shape), F32),
                        pltpu.SemaphoreType.DMA((2,)), pltpu.SemaphoreType.DMA((1,))],
        compiler_params=_params("arbitrary"), name=name + "_cast",
    )(x, g.reshape(1, d), w)


def _key_rows_to_cache(k_rows, b, t, n_heads, dh):
    return k_rows.reshape(b, t, n_heads, 2, dh)


def _value_rows_to_cache(v_rows, b, t, n_heads, dh):
    v = v_rows.reshape(b, t, 2, n_heads, dh)
    return jnp.swapaxes(v, 2, 3).reshape(b, t, n_heads, 2 * dh)


def _key_cache_to_rows(k_cache):
    b, t, n_heads, _, dh = k_cache.shape
    return k_cache.reshape(b, t * n_heads * 2, dh)


def _value_cache_to_rows(v_cache):
    b, t, n_heads, dv = v_cache.shape
    v = v_cache.reshape(b, t, n_heads, 2, dv // 2)
    return jnp.swapaxes(v, 2, 3).reshape(b, t * 2 * n_heads, dv // 2)


def _da_lambda(lq1_ref, lk1_ref, lq2_ref, lk2_ref, lam_init):
    s1 = jnp.sum(lq1_ref[...] * lk1_ref[...], axis=1, keepdims=True)
    s2 = jnp.sum(lq2_ref[...] * lk2_ref[...], axis=1, keepdims=True)
    return jnp.exp(s1) - jnp.exp(s2) + lam_init


def _da_head_out(o, sub_ref, lam_init):
    return (_rmsnorm(o, sub_ref[...], DA_SUBLN_EPS) * (1.0 - lam_init)).astype(BF16)


def _da_attn_kernel(lq1_ref, lk1_ref, lq2_ref, lk2_ref, q_ref, k_ref, vt_ref, x_ref, wo_ref,
                    sub_ref, o_ref, acc_ref, cat_ref, *, n_heads, dh, tq, lam_init):
    i = pl.program_id(1)
    dv = 2 * dh
    qn = tq // 2
    n_chains = 2 * n_heads
    lam = _da_lambda(lq1_ref, lk1_ref, lq2_ref, lk2_ref, lam_init)
    key_chunk = lax.broadcasted_iota(jnp.int32, (qn, qn), 0) // CHUNK
    qry_chunk = lax.broadcasted_iota(jnp.int32, (qn, qn), 1) // CHUNK
    diag_mask = key_chunk <= qry_chunk
    ones = jnp.ones((SUM_ROWS, tq), BF16)

    acc_ref[...] = jnp.zeros_like(acc_ref)

    def run_pieces(pieces, stats):
        stats = dict(stats)

        def scores(piece):
            chain, q0, qw, k0, kn, _ = piece
            cols = slice(chain * dh, (chain + 1) * dh)
            return _dot_nt(k_ref[0, pl.ds(k0, kn), cols], q_ref[0, q0:q0 + qw, cols])

        pending = [scores(p) for p in pieces[:SCORE_LOOKAHEAD]]
        for n, (chain, q0, qw, k0, kn, masked) in enumerate(pieces):
            s = pending.pop(0)
            if n + SCORE_LOOKAHEAD < len(pieces):
                pending.append(scores(pieces[n + SCORE_LOOKAHEAD]))
            if masked == qw:
                s = jnp.where(diag_mask, s, NEG_INF)
            elif masked:
                s = jnp.concatenate([jnp.where(diag_mask, s[:, :masked], NEG_INF), s[:, masked:]],
                                    axis=1)
            m_old = stats[chain, q0]
            m_new = jnp.maximum(m_old, jnp.max(s, axis=0, keepdims=True))
            alpha = jnp.exp2(m_old - m_new)
            p = jnp.exp2(s - m_new).astype(BF16)
            h = chain // 2
            vt = jnp.concatenate([vt_ref[h * dv:(h + 1) * dv, pl.ds(k0, kn)], ones[:, :kn]], axis=0)
            lanes = slice(q0, q0 + qw)
            acc_ref[chain, :, lanes] = alpha * acc_ref[chain, :, lanes] + _dot(vt, p)
            stats[chain, q0] = m_new
        return stats

    def full_block(j, maxima):
        k0 = pl.multiple_of(j * tq, tq)
        stats = run_pieces([(c, 0, tq, k0, tq, 0) for c in range(n_chains)],
                           {(c, 0): maxima[c] for c in range(n_chains)})
        return tuple(stats[c, 0] for c in range(n_chains))

    m0 = jnp.full((1, tq), NEG_INF, F32)
    maxima = lax.fori_loop(0, i, full_block, (m0,) * n_chains)

    k_diag = pl.multiple_of(i * tq, tq)
    stats = run_pieces([(c, 0, tq, k_diag, qn, qn) for c in range(n_chains)],
                       {(c, 0): maxima[c] for c in range(n_chains)})
    run_pieces([(c, qn, qn, k_diag + qn, qn, qn) for c in range(n_chains)],
               {(c, qn): stats[c, 0][:, qn:] for c in range(n_chains)})

    for h in range(n_heads):
        l1, l2 = acc_ref[2 * h, dv:dv + 1, :], acc_ref[2 * h + 1, dv:dv + 1, :]
        o = acc_ref[2 * h, :dv, :] / l1 - lam * (acc_ref[2 * h + 1, :dv, :] / l2)
        on = o * lax.rsqrt(jnp.mean(o * o, axis=0, keepdims=True) + DA_SUBLN_EPS) * sub_ref[...]
        cat_ref[h * dv:(h + 1) * dv, :] = (on * (1.0 - lam_init)).astype(BF16)

    o_ref[0] = x_ref[0] + _dot_tn(cat_ref[...], wo_ref[...])


def _da_attn_prompt(lams, q, kb, vt, x, wo, subln, n_heads, lam_init):
    b, t, d = x.shape
    dh = d // (2 * n_heads)
    tq = ATTN_TILE if t % ATTN_TILE == 0 else t
    assert tq % CHUNK == 0
    blk = pl.BlockSpec((1, tq, d), lambda bi, i: (bi, i, 0))
    seq = pl.BlockSpec((1, t, d), lambda bi, i: (bi, 0, 0))
    seq_t = pl.BlockSpec((d, t), lambda bi, i: (0, bi))
    lam_spec = _resident((1, dh))
    return pl.pallas_call(
        functools.partial(_da_attn_kernel, n_heads=n_heads, dh=dh, tq=tq, lam_init=lam_init),
        out_shape=jax.ShapeDtypeStruct((b, t, d), F32),
        grid=(b, t // tq),
        in_specs=[lam_spec] * 4 + [blk, seq, seq_t, blk, _resident((d, d)),
                                   _resident((2 * dh, 1))],
        out_specs=blk,
        scratch_shapes=[pltpu.VMEM((2 * n_heads, 2 * dh + SUM_ROWS, tq), F32),
                        pltpu.VMEM((d, tq), BF16)],
        compiler_params=pltpu.CompilerParams(dimension_semantics=("parallel", "arbitrary"),
                                             vmem_limit_bytes=VMEM_LIMIT_LARGE_BYTES),
        name="da_attn_prompt",
    )(*lams, q, kb, vt, x, wo, subln.reshape(2 * dh, 1))


def _da_attn_sample_kernel(lq1_ref, lk1_ref, lq2_ref, lk2_ref, q_ref, kc_ref, vc_ref, kn_ref,
                           vn_ref, x_ref, wo_ref, sub_ref, o_ref, cat_ref, *, n_heads, dh,
                           lam_init):
    lam = _da_lambda(lq1_ref, lk1_ref, lq2_ref, lk2_ref, lam_init)
    rows_per_token = 2 * n_heads
    past = kc_ref.shape[1] // rows_per_token

    def cached(ref, row):
        return ref.at[0][pl.ds(row, past, stride=rows_per_token), :].astype(BF16)

    scores = []
    for chain in range(2 * n_heads):
        cols = slice(chain * dh, (chain + 1) * dh)
        qh = q_ref[0, :, cols]
        scores.append((_dot_nt(qh, cached(kc_ref, chain)), _dot_nt(qh, kn_ref[0, :, cols])))
    probs = []
    for sc, sn in scores:
        m = jnp.maximum(jnp.max(sc, axis=1, keepdims=True), jnp.max(sn, axis=1, keepdims=True))
        pc = jnp.exp2(sc - m)
        pn = jnp.exp2(sn - m)
        l = jnp.sum(pc, axis=1, keepdims=True) + jnp.sum(pn, axis=1, keepdims=True)
        probs.append((pc / l, pn / l))
    for h in range(n_heads):
        c0 = h * 2 * dh
        ac = (probs[2 * h][0] - lam * probs[2 * h + 1][0]).astype(BF16)
        an = (probs[2 * h][1] - lam * probs[2 * h + 1][1]).astype(BF16)
        vc = jnp.concatenate([cached(vc_ref, h), cached(vc_ref, n_heads + h)], axis=1)
        o = _dot(ac, vc) + _dot(an, vn_ref[0, :, c0:c0 + 2 * dh])
        cat_ref[:, c0:c0 + 2 * dh] = _da_head_out(o, sub_ref, lam_init)
    o_ref[0] = x_ref[0] + _dot(cat_ref[...], wo_ref[...])


def _da_attn_sample(lams, q, k_rows, v_rows, kb, vb, x, wo, subln, n_heads, lam_init):
    b, ts, d = x.shape
    dh = d // (2 * n_heads)
    past = k_rows.shape[1] // (2 * n_heads)
    assert past % CHUNK == 0 and ts <= CHUNK
    new = pl.BlockSpec((1, ts, d), lambda bi: (bi, 0, 0))
    cache = pl.BlockSpec((1, past * 2 * n_heads, dh), lambda bi: (bi, 0, 0))
    lam_spec = _resident((1, dh))
    return pl.pallas_call(
        functools.partial(_da_attn_sample_kernel, n_heads=n_heads, dh=dh, lam_init=lam_init),
        out_shape=jax.ShapeDtypeStruct((b, ts, d), F32),
        grid=(b,),
        in_specs=[lam_spec] * 4 + [new, cache, cache, new, new, new, _resident((d, d)),
                                   _resident((1, 2 * dh))],
        out_specs=new,
        scratch_shapes=[pltpu.VMEM((ts, d), BF16)],
        compiler_params=_params("parallel"),
        name="da_attn_sample",
    )(*lams, q, k_rows, v_rows, kb, vb, x, wo, subln.reshape(1, 2 * dh))


def _rope_tables(first_pos, n_pos, dk):
    angle = np.repeat(1.0 / (ROPE_BASE ** np.linspace(0.0, 1.0, dk // 2)), 2)
    theta = (first_pos + np.arange(n_pos, dtype=np.float64))[:, None] * angle[None, :]
    sign = np.where(np.arange(dk) % 2 == 0, -1.0, 1.0)
    return jnp.asarray(np.cos(theta), F32), jnp.asarray(np.sin(theta) * sign[None, :], F32)


def _ret_proj_kernel(x_ref, g_ref, w_ref, cos_ref, sin_ref, q_ref, k_ref, v_ref, gate_ref, *,
                     n_heads, dk, k_scale):
    nq = n_heads * dk
    nv = v_ref.shape[1]
    h = _rmsnorm(x_ref[...], g_ref[...], NORM_EPS).astype(BF16)
    cos = cos_ref[...]
    sin = sin_ref[...]
    even = lax.broadcasted_iota(jnp.int32, cos.shape, 1) % 2 == 0

    def rotary(a):
        rot = jnp.where(even, pltpu.roll(a, dk - 1, 1), pltpu.roll(a, 1, 1))
        return a * cos + rot * sin

    q = _dot(h, w_ref[:, 0:nq])
    k = _dot(h, w_ref[:, nq:2 * nq])
    for hd in range(n_heads):
        sl = slice(hd * dk, (hd + 1) * dk)
        q_ref[:, sl] = rotary(q[:, sl]).astype(BF16)
        k_ref[:, sl] = (rotary(k[:, sl]) * k_scale).astype(BF16)
    v_ref[...] = _dot(h, w_ref[:, 2 * nq:2 * nq + nv]).astype(BF16)
    gate_ref[...] = _dot(h, w_ref[:, 2 * nq + nv:2 * nq + 2 * nv])


def _ret_proj(x, g, w, cos, sin, n_heads, dk, dv):
    n, d = x.shape
    nq, nv = n_heads * dk, n_heads * dv
    tm = _row_tile(cos.shape[0])
    pos_blocks = cos.shape[0] // tm
    row = lambda width: pl.BlockSpec((tm, width), lambda i: (i, 0))
    tab = pl.BlockSpec((tm, dk), lambda i: (i % pos_blocks, 0))
    return pl.pallas_call(
        functools.partial(_ret_proj_kernel, n_heads=n_heads, dk=dk, k_scale=dk ** -0.5),
        out_shape=(jax.ShapeDtypeStruct((n, nq), BF16), jax.ShapeDtypeStruct((n, nq), BF16),
                   jax.ShapeDtypeStruct((n, nv), BF16), jax.ShapeDtypeStruct((n, nv), F32)),
        grid=(n // tm,),
        in_specs=[row(d), _resident((1, d)), _resident((d, 2 * nq + 2 * nv)), tab, tab],
        out_specs=(row(nq), row(nq), row(nv), row(nv)),
        compiler_params=_params("parallel"),
        name="ret_proj",
    )(x, g.reshape(1, d), w, cos, sin)


def _ret_core_kernel(lg_ref, q_ref, k_ref, v_ref, gate_ref, x_ref, s0_ref, wo_ref, o_ref,
                     sfin_ref, s_scr, y_scr, decay_scr, *, n_heads, dk, dv, blk):
    c = pl.program_id(1)

    @pl.when(c == 0)
    def _():
        s_scr[...] = s0_ref[0]
        _fill_decay(decay_scr, lg_ref, n_heads, blk)

    heads = lambda ref, width: (lambda h: ref[0, :, h * width:(h + 1) * width])
    _ret_block(lg_ref, heads(q_ref, dk), heads(k_ref, dk), heads(v_ref, dv), heads(gate_ref, dv),
               s_scr, y_scr, decay_scr, n_heads, dv, blk)
    o_ref[0] = x_ref[0] + _dot(y_scr[...], wo_ref[...])

    @pl.when(c == pl.num_programs(1) - 1)
    def _():
        sfin_ref[0] = s_scr[...]


def _fill_decay(decay_scr, lg_ref, n_heads, blk):
    row = lax.broadcasted_iota(jnp.int32, (blk, blk), 0)
    col = lax.broadcasted_iota(jnp.int32, (blk, blk), 1)
    rel = (row - col).astype(F32)
    for h in range(n_heads):
        decay_scr[h] = jnp.where(rel >= 0, jnp.exp(jnp.maximum(rel, 0.0) * lg_ref[h]), 0.0)


def _ret_block(lg_ref, q_of, k_of, v_of, gate_of, s_scr, y_ref, decay_scr, n_heads, dv, blk):
    idx = lax.broadcasted_iota(jnp.int32, (blk, 1), 0).astype(F32)
    first = []
    for h in range(n_heads):
        lg = lg_ref[h]
        qh, kh, vh = q_of(h), k_of(h), v_of(h)
        s = s_scr[h]
        qk = _dot_nt(qh, kh)
        cross = _dot(qh, s.astype(BF16))
        kd = (kh.astype(F32) * jnp.exp((blk - 1.0 - idx) * lg)).astype(BF16)
        s_decay = jnp.exp(jnp.full((1, dv), float(blk), F32) * lg)
        s_scr[h] = s * s_decay + _dot_tn(kd, vh)
        first.append((qk, cross))

    for h, (qk, cross) in enumerate(first):
        inner = _dot((qk * decay_scr[h]).astype(BF16), v_of(h))
        o = inner + cross * jnp.exp((idx + 1.0) * lg_ref[h])
        on = o * lax.rsqrt(jnp.mean(o * o, axis=-1, keepdims=True) + RET_GN_EPS)
        gate = gate_of(h)
        y_ref[:, h * dv:(h + 1) * dv] = (gate * jax.nn.sigmoid(gate) * on).astype(BF16)


def _ret_fused_kernel(lg_ref, x_ref, g_ref, win_ref, cos_ref, sin_ref, wo_ref, o_ref, sfin_ref,
                      s_scr, y_scr, decay_scr, qkv_scr, gate_scr, *, n_heads, dk, dv, blk, k_scale):
    c = pl.program_id(1)
    nq, nv = n_heads * dk, n_heads * dv
    rows = x_ref.shape[1]

    @pl.when(c == 0)
    def _():
        s_scr[...] = jnp.zeros_like(s_scr)
        _fill_decay(decay_scr, lg_ref, n_heads, blk)

    h_in = _rmsnorm(x_ref[0], g_ref[...], NORM_EPS).astype(BF16)
    cos, sin = cos_ref[...], sin_ref[...]
    even = lax.broadcasted_iota(jnp.int32, cos.shape, 1) % 2 == 0

    def rotary(a):
        rot = jnp.where(even, pltpu.roll(a, dk - 1, 1), pltpu.roll(a, 1, 1))
        return a * cos + rot * sin

    q = _dot(h_in, win_ref[:, 0:nq])
    k = _dot(h_in, win_ref[:, nq:2 * nq])
    qkv_scr[:, 2 * nq:] = _dot(h_in, win_ref[:, 2 * nq:2 * nq + nv]).astype(BF16)
    gate_scr[...] = _dot(h_in, win_ref[:, 2 * nq + nv:2 * nq + 2 * nv])
    for hd in range(n_heads):
        sl = slice(hd * dk, (hd + 1) * dk)
        qkv_scr[:, sl] = rotary(q[:, sl]).astype(BF16)
        qkv_scr[:, nq + hd * dk:nq + (hd + 1) * dk] = (rotary(k[:, sl]) * k_scale).astype(BF16)

    for r0 in range(0, rows, blk):
        tok = slice(r0, r0 + blk)
        _ret_block(lg_ref,
                   lambda h, tok=tok: qkv_scr[tok, h * dk:(h + 1) * dk],
                   lambda h, tok=tok: qkv_scr[tok, nq + h * dk:nq + (h + 1) * dk],
                   lambda h, tok=tok: qkv_scr[tok, 2 * nq + h * dv:2 * nq + (h + 1) * dv],
                   lambda h, tok=tok: gate_scr[tok, h * dv:(h + 1) * dv],
                   s_scr, y_scr.at[tok], decay_scr, n_heads, dv, blk)
    o_ref[0] = x_ref[0] + _dot(y_scr[...], wo_ref[...])

    @pl.when(c == pl.num_programs(1) - 1)
    def _():
        sfin_ref[0] = s_scr[...]


def _ret_fused(lg, x, g, w_in, cos, sin, wo, n_heads, dk, dv):
    b, t, d = x.shape
    rows, blk = RET_FUSED_ROWS, RET_TILE
    assert t % rows == 0 and rows % blk == 0
    nq, nv = n_heads * dk, n_heads * dv
    tok = pl.BlockSpec((1, rows, d), lambda bi, c: (bi, c, 0))
    tab = pl.BlockSpec((rows, dk), lambda bi, c: (c, 0))
    state = pl.BlockSpec((1, n_heads, dk, dv), lambda bi, c: (bi, 0, 0, 0))
    return pl.pallas_call(
        functools.partial(_ret_fused_kernel, n_heads=n_heads, dk=dk, dv=dv, blk=blk,
                          k_scale=dk ** -0.5),
        out_shape=(jax.ShapeDtypeStruct((b, t, d), F32),
                   jax.ShapeDtypeStruct((b, n_heads, dk, dv), F32)),
        grid=(b, t // rows),
        in_specs=[pl.BlockSpec(memory_space=pltpu.SMEM), tok, _resident((1, d)),
                  _resident((d, 2 * nq + 2 * nv)), tab, tab, _resident((nv, d))],
        out_specs=(tok, state),
        scratch_shapes=[pltpu.VMEM((n_heads, dk, dv), F32), pltpu.VMEM((rows, nv), BF16),
                        pltpu.VMEM((n_heads, blk, blk), F32),
                        pltpu.VMEM((rows, 2 * nq + nv), BF16), pltpu.VMEM((rows, nv), F32)],
        compiler_params=_params("parallel", "arbitrary"),
        name="ret_fused",
    )(lg, x, g.reshape(1, d), w_in, cos, sin, wo)


def _ret_core(lg, q, k, v, gate, x, s0, wo, n_heads, dk, dv):
    b, t, d = x.shape
    blk = RET_TILE if t % RET_TILE == 0 else t
    nq, nv = n_heads * dk, n_heads * dv
    tok = lambda width: pl.BlockSpec((1, blk, width), lambda bi, c: (bi, c, 0))
    state = pl.BlockSpec((1, n_heads, dk, dv), lambda bi, c: (bi, 0, 0, 0))
    return pl.pallas_call(
        functools.partial(_ret_core_kernel, n_heads=n_heads, dk=dk, dv=dv, blk=blk),
        out_shape=(jax.ShapeDtypeStruct((b, t, d), F32),
                   jax.ShapeDtypeStruct((b, n_heads, dk, dv), F32)),
        grid=(b, t // blk),
        in_specs=[pl.BlockSpec(memory_space=pltpu.SMEM), tok(nq), tok(nq), tok(nv), tok(nv),
                  tok(d), state, _resident((nv, d))],
        out_specs=(tok(d), state),
        scratch_shapes=[pltpu.VMEM((n_heads, dk, dv), F32), pltpu.VMEM((blk, nv), BF16),
                        pltpu.VMEM((n_heads, blk, blk), F32)],
        compiler_params=_params("parallel", "arbitrary"),
        name="ret_core",
    )(lg, q, k, v, gate, x, s0, wo)


def _lambda_init(layer_idx):
    return 0.8 - 0.6 * math.exp(-0.3 * layer_idx)


def kernel(x_prompt, x_sample, cache_diff_k, cache_diff_v, state_ret, ffn1_norm, ffn1_w_gate, ffn1_w_up, ffn1_w_down, mix_norm, da_w_qkv, da_lambda_q1, da_lambda_k1, da_lambda_q2, da_lambda_k2, da_subln, da_w_o, ret_w_in, ret_w_o, ffn2_norm, ffn2_w_gate, ffn2_w_up, ffn2_w_down, final_norm):
    bp, tp, d = x_prompt.shape
    bs, ts, _ = x_sample.shape
    past = cache_diff_k.shape[2]
    da_heads = cache_diff_k.shape[3]
    dh = cache_diff_k.shape[5]
    ret_heads, dk, dv = state_ret.shape[2:]
    depth = ffn1_norm.shape[0]
    bf = lambda w: w.astype(BF16)

    xp = x_prompt.reshape(bp * tp, d)
    xs = x_sample.reshape(bs * ts, d)
    kp_list, vp_list, sp_list = [], [], []
    ks_list, vs_list, ss_list = [], [], []
    for i in range(depth):
        xp, xs = _ffn(xp, xs, ffn1_norm[i], ffn1_w_gate, ffn1_w_up, ffn1_w_down, layer=i)
        if i % 2 == 0:
            a = i // 2
            lam_init = _lambda_init(i)
            w_o = bf(da_w_o[a])
            lams = [v[a].reshape(1, dh) for v in (da_lambda_q1, da_lambda_k1, da_lambda_q2, da_lambda_k2)]
            q_scale = dh ** -0.5 * math.log2(math.e)
            qp, kp, vp, kpb, vpt, w_qkv = _da_proj(xp, mix_norm[i], da_w_qkv, q_scale, True,
                                                   da_heads, layer=a)
            qs, ks, vs, ksb, vsb = _da_proj(xs, mix_norm[i], w_qkv, q_scale, False, da_heads)
            seq = lambda z, b, t: z.reshape(b, t, d)
            xp = _da_attn_prompt(lams, seq(qp, bp, tp), seq(kpb, bp, tp), vpt,
                                 seq(xp, bp, tp), w_o, da_subln[a], da_heads,
                                 lam_init).reshape(bp * tp, d)
            xs = _da_attn_sample(lams, seq(qs, bs, ts), _key_cache_to_rows(cache_diff_k[a]),
                                 _value_cache_to_rows(cache_diff_v[a]), seq(ksb, bs, ts),
                                 seq(vsb, bs, ts), seq(xs, bs, ts), w_o, da_subln[a], da_heads,
                                 lam_init).reshape(bs * ts, d)
            kp_list.append(_key_rows_to_cache(kp, bp, tp, da_heads, dh))
            vp_list.append(_value_rows_to_cache(vp, bp, tp, da_heads, dh))
            ks_list.append(_key_rows_to_cache(ks, bs, ts, da_heads, dh))
            vs_list.append(_value_rows_to_cache(vs, bs, ts, da_heads, dh))
        else:
            r = i // 2
            w_in, w_o = bf(ret_w_in[r]), bf(ret_w_o[r])
            lg = jnp.log1p(-jnp.exp2(-5.0 - jnp.arange(ret_heads, dtype=F32)))
            nq, nv = ret_heads * dk, ret_heads * dv
            cos, sin = _rope_tables(0, tp, dk)
            yp, sp = _ret_fused(lg, xp.reshape(bp, tp, d), mix_norm[i], w_in, cos, sin, w_o,
                                ret_heads, dk, dv)
            xp = yp.reshape(bp * tp, d)
            sp_list.append(sp)
            cos, sin = _rope_tables(past, ts, dk)
            cos, sin = jnp.tile(cos, (bs, 1)), jnp.tile(sin, (bs, 1))
            q, k, v, gate = _ret_proj(xs, mix_norm[i], w_in, cos, sin, ret_heads, dk, dv)
            ys, ss = _ret_core(lg, q.reshape(bs, ts, nq), k.reshape(bs, ts, nq),
                               v.reshape(bs, ts, nv), gate.reshape(bs, ts, nv),
                               xs.reshape(bs, ts, d), state_ret[r].astype(F32), w_o, ret_heads,
                               dk, dv)
            xs = ys.reshape(bs * ts, d)
            ss_list.append(ss)
        fin = final_norm if i == depth - 1 else None
        xp, xs = _ffn(xp, xs, ffn2_norm[i], ffn2_w_gate, ffn2_w_up, ffn2_w_down, layer=i,
                      final_g=fin)

    return (xp.reshape(bp, tp, d), xs.reshape(bs, ts, d),
            jnp.stack(kp_list), jnp.stack(vp_list), jnp.stack(sp_list),
            jnp.stack(ks_list), jnp.stack(vs_list), jnp.stack(ss_list))
```
